```python
import jax, jax.numpy as jnp
from jax import lax
import numpy as np

D_MODEL = 2048
BATCH = 1
SEQ = 8192
DEPTH = 2

CHUNK = 64
N_MEM = 256
NORM_EPS = 1e-6
GN_EPS = 64e-5

RWKV_WIDTH = D_MODEL // 2
RWKV_HEAD_DIM = 64
RWKV_HEADS = RWKV_WIDTH // RWKV_HEAD_DIM
DECAY_LORA = 96
AAA_LORA = 96
GATE_LORA = 256
MLSTM_WIDTH = D_MODEL // 2
MLSTM_HEAD_DIM = 256
MLSTM_HEADS = MLSTM_WIDTH // MLSTM_HEAD_DIM
MLSTM_CONV = 4
FOX_WIDTH = D_MODEL
FOX_HEAD_DIM = 128
FOX_HEADS = FOX_WIDTH // FOX_HEAD_DIM
Q_BLOCK = 128
XATTN_HEADS = 4
XATTN_HEAD_DIM = D_MODEL // XATTN_HEADS
D_FF = 11 * D_MODEL // 4
N_EXPERTS = 8
TOP_K = 2
N_EVEN = (DEPTH + 1) // 2
N_ODD = DEPTH // 2

A_SIZES = (RWKV_WIDTH, RWKV_WIDTH, RWKV_WIDTH, DECAY_LORA, AAA_LORA, GATE_LORA)
B_SIZES = (2 * MLSTM_WIDTH, MLSTM_WIDTH, MLSTM_WIDTH, 2 * MLSTM_HEADS)
A_COLS = sum(A_SIZES)
B_COLS = sum(B_SIZES)
FOX_SIZES = (FOX_WIDTH, FOX_WIDTH, FOX_WIDTH, FOX_WIDTH, FOX_HEADS)
FOX_COLS = sum(FOX_SIZES)

kernel_name = "hybrid_rwkv7_mlstm_fox_moe_trunk"


def _split(t, sizes):
    offsets = [int(o) for o in np.cumsum(sizes)[:-1]]
    return jnp.split(t, offsets, axis=-1)


def _rmsnorm(x, g, eps=NORM_EPS):
    xf = x.astype(jnp.float32)
    y = xf * lax.rsqrt(jnp.mean(xf * xf, axis=-1, keepdims=True) + eps)
    return (y * g).astype(x.dtype)


def _token_shift(p, mu):
    prev = jnp.pad(p, ((0, 0), (1, 0), (0, 0)))[:, :-1]
    return p + (prev - p) * mu


def _causal_conv(x, w):
    width, ch = w.shape
    return lax.conv_general_dilated(x, w[:, None, :].astype(x.dtype), window_strides=(1,),
                                    padding=[(width - 1, 0)],
                                    dimension_numbers=('NWC', 'WIO', 'NWC'),
                                    feature_group_count=ch)


def _rwkv7_time_mix(r, k, v, w_lo, a_lo, g_lo, w0, w_up, a0, a_up, g_up, k_k, k_a, r_k, ln_g, ln_b):
    bsz, seq, width = r.shape
    f32 = jnp.float32
    log_w = -jax.nn.softplus(-(w0 + jnp.tanh(w_lo) @ w_up)) - 0.5
    decay = jnp.exp(-jnp.exp(log_w.astype(f32)))
    a = jax.nn.sigmoid(a0 + a_lo @ a_up).astype(f32)
    g = jax.nn.sigmoid(g_lo) @ g_up
    heads = lambda t: t.astype(f32).reshape(bsz, seq, RWKV_HEADS, RWKV_HEAD_DIM)
    kk = heads(k * k_k)
    kk = kk / jnp.maximum(jnp.sqrt(jnp.sum(kk * kk, axis=-1, keepdims=True)), 1e-12)
    k_mod = heads(k.astype(f32) * (1.0 + (a - 1.0) * k_a))
    rh, vh, wh, ah = heads(r), heads(v), heads(decay), heads(a)
    b = kk * ah

    def step(state, inp):
        r_t, w_t, k_t, v_t, kk_t, b_t = inp
        sa = jnp.einsum('bhvk,bhk->bhv', state, kk_t)
        state = (state * w_t[:, :, None, :] - sa[..., None] * b_t[:, :, None, :]
                 + v_t[..., None] * k_t[:, :, None, :])
        return state, jnp.einsum('bhvk,bhk->bhv', state, r_t)

    s0 = jnp.zeros((bsz, RWKV_HEADS, RWKV_HEAD_DIM, RWKV_HEAD_DIM), f32)
    xs = tuple(jnp.moveaxis(t, 1, 0) for t in (rh, wh, k_mod, vh, kk, b))
    _, y = lax.scan(step, s0, xs)
    y = jnp.moveaxis(y, 0, 1)
    mean = jnp.mean(y, axis=-1, keepdims=True)
    var = jnp.mean(jnp.square(y - mean), axis=-1, keepdims=True)
    y = ((y - mean) * lax.rsqrt(var + GN_EPS)).reshape(bsz, seq, width) * ln_g + ln_b
    bonus = jnp.sum(rh * k_mod * r_k.reshape(RWKV_HEADS, RWKV_HEAD_DIM), axis=-1, keepdims=True) * vh
    y = y + bonus.reshape(bsz, seq, width)
    return (y * g).astype(r.dtype)


def _mlstm_chunkwise(q, k, v, i_pre, f_pre):
    bsz, seq, nh, dh = q.shape
    n_chunks = seq // CHUNK

    def chunks(t):
        t = t.reshape((bsz, n_chunks, CHUNK) + t.shape[2:])
        return jnp.moveaxis(t, (1, 2), (0, 3))

    qc, kc, vc = chunks(q * dh ** -0.5), chunks(k), chunks(v)
    b_cum = jnp.cumsum(chunks(jax.nn.log_sigmoid(f_pre)), axis=-1)
    ig = chunks(i_pre)
    causal = jnp.tril(jnp.ones((CHUNK, CHUNK), dtype=bool))

    def step(carry, inp):
        c_mat, n_vec, m = carry
        q_c, k_c, v_c, b_c, i_c = inp
        d_mat = jnp.where(causal, b_c[..., :, None] - b_c[..., None, :] + i_c[..., None, :], -jnp.inf)
        inter = b_c + m[..., None]
        m_row = jnp.maximum(inter, jnp.max(d_mat, axis=-1))
        w_inter = jnp.exp(inter - m_row)
        s = jnp.einsum('bhld,bhsd->bhls', q_c, k_c) * jnp.exp(d_mat - m_row[..., None])
        num = (jnp.einsum('bhls,bhsd->bhld', s, v_c)
               + w_inter[..., None] * jnp.einsum('bhvk,bhlk->bhlv', c_mat, q_c))
        den = jnp.sum(s, axis=-1) + w_inter * jnp.einsum('bhk,bhlk->bhl', n_vec, q_c)
        h = num / jnp.maximum(jnp.abs(den), jnp.exp(-m_row))[..., None]
        b_last = b_c[..., -1]
        d_state = b_last[..., None] - b_c + i_c
        m_new = jnp.maximum(b_last + m, jnp.max(d_state, axis=-1))
        w_state = jnp.exp(d_state - m_new[..., None])
        w_carry = jnp.exp(b_last + m - m_new)
        c_mat = w_carry[..., None, None] * c_mat + jnp.einsum('bhs,bhsv,bhsk->bhvk', w_state, v_c, k_c)
        n_vec = w_carry[..., None] * n_vec + jnp.einsum('bhs,bhsk->bhk', w_state, k_c)
        return (c_mat, n_vec, m_new), h

    init = (jnp.zeros((bsz, nh, dh, dh), jnp.float32), jnp.zeros((bsz, nh, dh), jnp.float32),
            jnp.zeros((bsz, nh), jnp.float32))
    _, h = lax.scan(step, init, (qc, kc, vc, b_cum, ig))
    return jnp.moveaxis(h, (0, 3), (1, 2)).reshape(bsz, seq, nh, dh)


def _rwkv_mlstm_mixer(h, w_in, mu, w0, w_up, a0, a_up, g_up, k_k, k_a, r_k, ln_g, ln_b,
                      conv_qk, b_if, mh_g, w_out):
    bsz, seq, _ = h.shape
    f32 = jnp.float32
    proj = h @ w_in
    p_a, p_b = proj[..., :A_COLS], proj[..., A_COLS:]
    r, k, v, w_lo, a_lo, g_lo = _split(_token_shift(p_a, mu), A_SIZES)
    y_a = _rwkv7_time_mix(r, k, v, w_lo, a_lo, g_lo, w0, w_up, a0, a_up, g_up, k_k, k_a, r_k, ln_g, ln_b)
    qk, v_b, o_b, gates = _split(p_b, B_SIZES)
    qk = jax.nn.silu(_causal_conv(qk, conv_qk))
    q_b, k_b = qk[..., :MLSTM_WIDTH], qk[..., MLSTM_WIDTH:]
    gates = (gates + b_if).astype(f32)
    i_pre, f_pre = gates[..., :MLSTM_HEADS], gates[..., MLSTM_HEADS:]
    heads = lambda t: t.astype(f32).reshape(bsz, seq, MLSTM_HEADS, MLSTM_HEAD_DIM)
    h_b = _mlstm_chunkwise(heads(q_b), heads(k_b), heads(v_b), i_pre, f_pre)
    h_b = _rmsnorm(h_b, mh_g.reshape(MLSTM_HEADS, MLSTM_HEAD_DIM)).reshape(bsz, seq, MLSTM_WIDTH)
    y_b = (h_b * jax.nn.sigmoid(o_b.astype(f32))).astype(h.dtype)
    return jnp.concatenate([y_a, y_b], axis=-1) @ w_out


def _fox_attention(h, w_in, b_f, qn_g, kn_g, w_out):
    bsz, seq, _ = h.shape
    f32 = jnp.float32
    q, k, v, g, f_pre = _split(h @ w_in, FOX_SIZES)
    split_heads = lambda t: t.reshape(bsz, seq, FOX_HEADS, FOX_HEAD_DIM)
    q = _rmsnorm(split_heads(q), qn_g).transpose(0, 2, 1, 3)
    k = _rmsnorm(split_heads(k), kn_g).transpose(0, 2, 1, 3)
    v = split_heads(v).transpose(0, 2, 1, 3)
    log_f = jax.nn.log_sigmoid((f_pre + b_f).astype(f32))
    cum = jnp.cumsum(log_f, axis=1).transpose(0, 2, 1)
    scale = FOX_HEAD_DIM ** -0.5
    q_pos = jnp.arange(Q_BLOCK)
    outs = []
    for blk in range(seq // Q_BLOCK):
        q0 = blk * Q_BLOCK
        k_end = q0 + Q_BLOCK
        logits = jnp.einsum('bhqd,bhkd->bhqk', q[:, :, q0:k_end], k[:, :, :k_end]).astype(f32) * scale
        logits = logits + cum[:, :, q0:k_end, None] - cum[:, :, None, :k_end]
        mask = (q0 + q_pos)[:, None] >= jnp.arange(k_end)[None, :]
        probs = jax.nn.softmax(jnp.where(mask, logits, -jnp.inf), axis=-1)
        outs.append(jnp.einsum('bhqk,bhkd->bhqd', probs.astype(v.dtype), v[:, :, :k_end]))
    o = jnp.concatenate(outs, axis=2).transpose(0, 2, 1, 3).reshape(bsz, seq, FOX_WIDTH)
    return (o * jax.nn.sigmoid(g)) @ w_out


def _cross_attention(h, mem_n, wq, wkv, wo):
    bsz, seq, _ = h.shape
    n_mem = mem_n.shape[1]
    q = (h @ wq).reshape(bsz, seq, XATTN_HEADS, XATTN_HEAD_DIM)
    k, v = jnp.split(mem_n @ wkv, 2, axis=-1)
    k = k.reshape(bsz, n_mem, XATTN_HEADS, XATTN_HEAD_DIM)
    v = v.reshape(bsz, n_mem, XATTN_HEADS, XATTN_HEAD_DIM)
    logits = jnp.einsum('bthd,bmhd->bhtm', q, k).astype(jnp.float32) * XATTN_HEAD_DIM ** -0.5
    probs = jax.nn.softmax(logits, axis=-1)
    o = jnp.einsum('bhtm,bmhd->bthd', probs.astype(v.dtype), v).reshape(bsz, seq, D_MODEL)
    return o @ wo


def _swiglu(h, w_gate, w_up, w_down):
    return (jax.nn.silu(h @ w_gate) * (h @ w_up)) @ w_down


def _moe_swiglu(h, w_router, w_gate, w_up, w_down):
    logits = (h @ w_router).astype(jnp.float32)
    top_val, top_idx = lax.top_k(logits, TOP_K)
    gates = jax.nn.softmax(top_val, axis=-1)
    combine = jnp.einsum('btk,btke->bte', gates,
                         jax.nn.one_hot(top_idx, N_EXPERTS, dtype=jnp.float32)).astype(h.dtype)
    out = jnp.zeros_like(h)
    for e in range(N_EXPERTS):
        out = out + combine[..., e:e + 1] * _swiglu(h, w_gate[e], w_up[e], w_down[e])
    return out


def setup_inputs(seed: int = 0) -> dict:
    key = jax.random.key(seed)
    ks = iter(jax.random.split(key, 48))
    f32 = jnp.float32
    D = D_MODEL
    out_scale = (3 * DEPTH) ** -0.5

    def normal(shape, scale):
        return jax.random.normal(next(ks), shape, f32) * scale

    def uniform(shape, lo, hi):
        return jax.random.uniform(next(ks), shape, f32, lo, hi)

    def gain(shape):
        return 1.0 + normal(shape, 0.02)

    inp = {}
    inp['x'] = normal((BATCH, SEQ, D), 1.0)
    inp['mem'] = normal((BATCH, N_MEM, D), 1.0)
    inp['mem_norm_g'] = gain((D,))
    inp['final_norm_g'] = gain((D,))
    inp['norm_mix_g'] = gain((DEPTH, D))
    inp['norm_xattn_g'] = gain((DEPTH, D))
    inp['norm_ffn_g'] = gain((DEPTH, D))
    inp['xa_wq'] = normal((DEPTH, D, D), D ** -0.5)
    inp['xa_wkv'] = normal((DEPTH, D, 2 * D), D ** -0.5)
    inp['xa_wo'] = normal((DEPTH, D, D), D ** -0.5 * out_scale)
    inp['ab_w_in'] = normal((N_EVEN, D, A_COLS + B_COLS), D ** -0.5)
    inp['ab_mu'] = uniform((N_EVEN, A_COLS), 0.0, 1.0)
    inp['rw_w0'] = uniform((N_EVEN, RWKV_WIDTH), -6.0, -1.0)
    inp['rw_w_up'] = normal((N_EVEN, DECAY_LORA, RWKV_WIDTH), 0.1 * DECAY_LORA ** -0.5)
    inp['rw_a0'] = normal((N_EVEN, RWKV_WIDTH), 0.1)
    inp['rw_a_up'] = normal((N_EVEN, AAA_LORA, RWKV_WIDTH), 0.1 * AAA_LORA ** -0.5)
    inp['rw_g_up'] = normal((N_EVEN, GATE_LORA, RWKV_WIDTH), GATE_LORA ** -0.5)
    inp['rw_k_k'] = 0.85 + normal((N_EVEN, RWKV_WIDTH), 0.02)
    inp['rw_k_a'] = gain((N_EVEN, RWKV_WIDTH))
    inp['rw_r_k'] = normal((N_EVEN, RWKV_WIDTH), 0.1)
    inp['rw_ln_g'] = gain((N_EVEN, RWKV_WIDTH))
    inp['rw_ln_b'] = normal((N_EVEN, RWKV_WIDTH), 0.02)
    inp['ml_conv'] = normal((N_EVEN, MLSTM_CONV, 2 * MLSTM_WIDTH), MLSTM_CONV ** -0.5)
    inp['ml_b_if'] = jnp.concatenate([normal((N_EVEN, MLSTM_HEADS), 0.1),
                                      uniform((N_EVEN, MLSTM_HEADS), 3.0, 6.0)], axis=-1)
    inp['ml_norm_g'] = gain((N_EVEN, MLSTM_WIDTH))
    inp['ab_w_out'] = normal((N_EVEN, D, D), D ** -0.5 * out_scale)
    inp['ffn_w_gate'] = normal((N_EVEN, D, D_FF), D ** -0.5)
    inp['ffn_w_up'] = normal((N_EVEN, D, D_FF), D ** -0.5)
    inp['ffn_w_down'] = normal((N_EVEN, D_FF, D), D_FF ** -0.5 * out_scale)
    inp['fox_w_in'] = normal((N_ODD, D, FOX_COLS), D ** -0.5)
    inp['fox_b_f'] = uniform((N_ODD, FOX_HEADS), 3.0, 6.0)
    inp['fox_qn_g'] = gain((N_ODD, FOX_HEAD_DIM))
    inp['fox_kn_g'] = gain((N_ODD, FOX_HEAD_DIM))
    inp['fox_w_out'] = normal((N_ODD, D, D), D ** -0.5 * out_scale)
    inp['moe_router'] = normal((N_ODD, D, N_EXPERTS), D ** -0.5)
    inp['moe_w_gate'] = normal((N_ODD, N_EXPERTS, D, D_FF), D ** -0.5)
    inp['moe_w_up'] = normal((N_ODD, N_EXPERTS, D, D_FF), D ** -0.5)
    inp['moe_w_down'] = normal((N_ODD, N_EXPERTS, D_FF, D), D_FF ** -0.5 * out_scale)
    return inp


def reference(x, mem, mem_norm_g, final_norm_g, norm_mix_g, norm_xattn_g, norm_ffn_g,
              xa_wq, xa_wkv, xa_wo,
              ab_w_in, ab_mu, rw_w0, rw_w_up, rw_a0, rw_a_up, rw_g_up, rw_k_k, rw_k_a, rw_r_k,
              rw_ln_g, rw_ln_b, ml_conv, ml_b_if, ml_norm_g, ab_w_out,
              ffn_w_gate, ffn_w_up, ffn_w_down,
              fox_w_in, fox_b_f, fox_qn_g, fox_kn_g, fox_w_out,
              moe_router, moe_w_gate, moe_w_up, moe_w_down):
    mem_n = _rmsnorm(mem, mem_norm_g)
    for layer in range(DEPTH):
        j = layer // 2
        h = _rmsnorm(x, norm_mix_g[layer])
        if layer % 2 == 0:
            x = x + _rwkv_mlstm_mixer(h, ab_w_in[j], ab_mu[j], rw_w0[j], rw_w_up[j], rw_a0[j], rw_a_up[j],
                                      rw_g_up[j], rw_k_k[j], rw_k_a[j], rw_r_k[j], rw_ln_g[j], rw_ln_b[j],
                                      ml_conv[j], ml_b_if[j], ml_norm_g[j], ab_w_out[j])
        else:
            x = x + _fox_attention(h, fox_w_in[j], fox_b_f[j], fox_qn_g[j], fox_kn_g[j], fox_w_out[j])
        x = x + _cross_attention(_rmsnorm(x, norm_xattn_g[layer]), mem_n,
                                 xa_wq[layer], xa_wkv[layer], xa_wo[layer])
        h = _rmsnorm(x, norm_ffn_g[layer])
        if layer % 2 == 0:
            x = x + _swiglu(h, ffn_w_gate[j], ffn_w_up[j], ffn_w_down[j])
        else:
            x = x + _moe_swiglu(h, moe_router[j], moe_w_gate[j], moe_w_up[j], moe_w_down[j])
    return _rmsnorm(x, final_norm_g)
```

```python
import functools

import jax
import jax.numpy as jnp
from jax import lax
from jax.experimental import pallas as pl
from jax.experimental.pallas import tpu as pltpu

F32 = jnp.float32
BF16 = jnp.bfloat16
HIGHEST = lax.Precision.HIGHEST

D_MODEL = 2048
CHUNK = 64
NORM_EPS = 1e-6
GN_EPS = 64e-5
RWKV_WIDTH = 1024
RWKV_HEAD_DIM = 64
RWKV_HEADS = 16
DECAY_LORA = 96
AAA_LORA = 96
GATE_LORA = 256
LORA_PAD = 128
MLSTM_WIDTH = 1024
MLSTM_HEAD_DIM = 256
MLSTM_HEADS = 4
MLSTM_CONV = 4
FOX_HEAD_DIM = 128
FOX_HEADS = 16
XATTN_HEADS = 4
XATTN_HEAD_DIM = 512
D_FF = 5632
N_EXPERTS = 8
TOP_K = 2

VMEM_LIMIT_BYTES = 56 * 1024 * 1024


def _params(*semantics):
    return pltpu.CompilerParams(dimension_semantics=semantics, vmem_limit_bytes=VMEM_LIMIT_BYTES)


def _dot(a, b, precision=None):
    return jnp.dot(a, b, preferred_element_type=F32, precision=precision)


def _dot_nt(a, b, precision=None):
    return lax.dot_general(a, b, (((1,), (1,)), ((), ())), preferred_element_type=F32, precision=precision)


def _dot_tn(a, b, precision=None):
    return lax.dot_general(a, b, (((0,), (0,)), ((), ())), preferred_element_type=F32, precision=precision)


def _norm_mm_body(*refs, has_norm, has_res, precision):
    it = iter(refs)
    x_ref = next(it)
    g_ref = next(it) if has_norm else None
    w_ref = next(it)
    res_ref = next(it) if has_res else None
    o_ref = next(it)
    xn_ref = next(it)

    @pl.when(pl.program_id(1) == 0)
    def _():
        x = x_ref[...].astype(F32)
        if has_norm:
            x = x * lax.rsqrt(jnp.mean(x * x, axis=-1, keepdims=True) + NORM_EPS) * g_ref[...]
        xn_ref[...] = x.astype(xn_ref.dtype)

    acc = _dot(xn_ref[...], w_ref[...].astype(xn_ref.dtype), precision)
    if has_res:
        acc = acc + res_ref[...]
    o_ref[...] = acc.astype(o_ref.dtype)


def _norm_mm(x, w, *, norm_g=None, res=None, tm=512, tn=512, out_dtype=F32, mxu_dtype=BF16, precision=None):
    m, k = x.shape
    n = w.shape[1]
    tm, tn = min(tm, m), min(tn, n)
    assert m % tm == 0
    in_specs = [pl.BlockSpec((tm, k), lambda i, j: (i, 0))]
    args = [x]
    if norm_g is not None:
        in_specs.append(pl.BlockSpec((1, k), lambda i, j: (0, 0)))
        args.append(norm_g.reshape(1, k))
    in_specs.append(pl.BlockSpec((k, tn), lambda i, j: (0, j)))
    args.append(w)
    if res is not None:
        in_specs.append(pl.BlockSpec((tm, tn), lambda i, j: (i, j)))
        args.append(res)
    return pl.pallas_call(
        functools.partial(_norm_mm_body, has_norm=norm_g is not None, has_res=res is not None, precision=precision),
        grid=(m // tm, pl.cdiv(n, tn)),
        in_specs=in_specs,
        out_specs=pl.BlockSpec((tm, tn), lambda i, j: (i, j)),
        out_shape=jax.ShapeDtypeStruct((m, n), out_dtype),
        scratch_shapes=[pltpu.VMEM((tm, k), mxu_dtype)],
        compiler_params=_params("arbitrary", "arbitrary"),
    )(*args)


def _rwkv_chunk_body(r_ref, k_ref, v_ref, wlo_ref, alo_ref, glo_ref,
                     w0_ref, wup_ref, a0_ref, aup_ref, gup_ref, kk_ref, ka_ref,
                     m_ref, n_ref, ry_ref, y0_ref, kmod_ref, g_ref):
    L, N = CHUNK, RWKV_HEAD_DIM
    r = r_ref[...]
    k = k_ref[...]
    v = v_ref[...]
    log_w = -jax.nn.softplus(-(w0_ref[...] + _dot(jnp.tanh(wlo_ref[...]), wup_ref[...], HIGHEST))) - 0.5
    lw = -jnp.exp(log_w)
    a = jax.nn.sigmoid(a0_ref[...] + _dot(alo_ref[...], aup_ref[...], HIGHEST))
    g_ref[...] = _dot(jax.nn.sigmoid(glo_ref[...]), gup_ref[...], HIGHEST)
    k_mod = k * (1.0 + (a - 1.0) * ka_ref[...])
    kmod_ref[...] = k_mod
    kk_raw = k * kk_ref[...]

    row = lax.broadcasted_iota(jnp.int32, (L, L), 0)
    col = lax.broadcasted_iota(jnp.int32, (L, L), 1)
    incl = col <= row
    strict = col < row
    eye = (col == row).astype(F32)
    cum = _dot(incl.astype(F32), lw, HIGHEST)
    cum_last = cum[L - 1:L, :]
    g_in = jnp.exp(cum)
    g_prev = jnp.exp(cum - lw)
    g_inv = jnp.exp(-cum)
    g_tail = jnp.exp(cum_last - cum)
    g_last = jnp.exp(cum_last)
    blk16 = (row // 16) == (col // 16)
    blk32 = (row // 32) == (col // 32)

    for h in range(RWKV_HEADS):
        sl = slice(h * N, (h + 1) * N)
        kk_h = kk_raw[:, sl]
        kk_h = kk_h / jnp.maximum(jnp.sqrt(jnp.sum(kk_h * kk_h, axis=-1, keepdims=True)), 1e-12)
        b_h = kk_h * a[:, sl]
        km_h = k_mod[:, sl]
        v_h = v[:, sl]
        kkg = kk_h * g_prev[:, sl]
        rg = r[:, sl] * g_in[:, sl]
        kd = km_h * g_inv[:, sl]
        bd = b_h * g_inv[:, sl]
        kdg = km_h * g_tail[:, sl]
        bdg = b_h * g_tail[:, sl]

        a_k = jnp.where(strict, _dot_nt(kkg, kd, HIGHEST), 0.0)
        a_b = jnp.where(strict, _dot_nt(kkg, bd, HIGHEST), 0.0)
        ar_k = jnp.where(incl, _dot_nt(rg, kd, HIGHEST), 0.0)
        ar_b = jnp.where(incl, _dot_nt(rg, bd, HIGHEST), 0.0)

        neg = jnp.where(blk16, -a_b, 0.0)
        t_inv = eye + neg
        pw = neg
        for _ in range(3):
            pw = _dot(pw, pw, HIGHEST)
            t_inv = t_inv + _dot(t_inv, pw, HIGHEST)
        off1 = jnp.where(blk32 & (~blk16), a_b, 0.0)
        t_inv = t_inv - _dot(_dot(t_inv, off1, HIGHEST), t_inv, HIGHEST)
        off2 = jnp.where(blk32, 0.0, a_b)
        t_inv = t_inv - _dot(_dot(t_inv, off2, HIGHEST), t_inv, HIGHEST)

        p = _dot(t_inv, kkg, HIGHEST)
        q = _dot(t_inv, _dot(a_k, v_h, HIGHEST), HIGHEST)
        ry_ref[:, sl] = rg - _dot(ar_b, p, HIGHEST)
        y0_ref[:, sl] = _dot(ar_k, v_h, HIGHEST) - _dot(ar_b, q, HIGHEST)
        m_ref[0, h] = eye * g_last[:, sl] - _dot_tn(bdg, p, HIGHEST)
        n_ref[0, h] = _dot_tn(kdg, v_h, HIGHEST) - _dot_tn(bdg, q, HIGHEST)


def _rwkv_scan_body(m_ref, n_ref, ry_ref, y0_ref, r_ref, kmod_ref, v_ref, g_ref, rk_ref, lng_ref, lnb_ref,
                    o_ref, h_ref):
    N = RWKV_HEAD_DIM

    @pl.when(pl.program_id(0) == 0)
    def _():
        h_ref[...] = jnp.zeros_like(h_ref)

    for h in range(RWKV_HEADS):
        sl = slice(h * N, (h + 1) * N)
        state = h_ref[h]
        y = _dot(ry_ref[:, sl], state, HIGHEST) + y0_ref[:, sl]
        h_ref[h] = _dot(m_ref[0, h], state, HIGHEST) + n_ref[0, h]
        mean = jnp.mean(y, axis=-1, keepdims=True)
        var = jnp.mean(jnp.square(y - mean), axis=-1, keepdims=True)
        y = (y - mean) * lax.rsqrt(var + GN_EPS) * lng_ref[:, sl] + lnb_ref[:, sl]
        v_h = v_ref[:, sl]
        bonus = jnp.sum(r_ref[:, sl] * kmod_ref[:, sl] * rk_ref[:, sl], axis=-1, keepdims=True) * v_h
        o_ref[:, sl] = ((y + bonus) * g_ref[:, sl]).astype(o_ref.dtype)


def _rwkv7(pa, lora, w0, w_up, a0, a_up, g_up, k_k, k_a, r_k, ln_g, ln_b, out_dtype=F32):
    t = pa.shape[0]
    nc = t // CHUNK
    W, L = RWKV_WIDTH, CHUNK
    c = 0
    row_w = lambda i: (i, 0)
    vec = lambda x: x.reshape(1, W)
    pad_rows = lambda x: jnp.pad(x, ((0, LORA_PAD - x.shape[0]), (0, 0)))
    full = lambda shape: pl.BlockSpec(shape, lambda i: (0,) * len(shape))
    m, n, ry, y0, kmod, g = pl.pallas_call(
        _rwkv_chunk_body,
        grid=(nc,),
        in_specs=[
            pl.BlockSpec((L, W), lambda i: (i, c)),
            pl.BlockSpec((L, W), lambda i: (i, c + 1)),
            pl.BlockSpec((L, W), lambda i: (i, c + 2)),
            pl.BlockSpec((L, LORA_PAD), lambda i: (i, 0)),
            pl.BlockSpec((L, LORA_PAD), lambda i: (i, 1)),
            pl.BlockSpec((L, GATE_LORA), lambda i: (i, 1)),
            full((1, W)), full((LORA_PAD, W)), full((1, W)), full((LORA_PAD, W)), full((GATE_LORA, W)),
            full((1, W)), full((1, W)),
        ],
        out_specs=[
            pl.BlockSpec((1, RWKV_HEADS, RWKV_HEAD_DIM, RWKV_HEAD_DIM), lambda i: (i, 0, 0, 0)),
            pl.BlockSpec((1, RWKV_HEADS, RWKV_HEAD_DIM, RWKV_HEAD_DIM), lambda i: (i, 0, 0, 0)),
            pl.BlockSpec((L, W), row_w), pl.BlockSpec((L, W), row_w),
            pl.BlockSpec((L, W), row_w), pl.BlockSpec((L, W), row_w),
        ],
        out_shape=[
            jax.ShapeDtypeStruct((nc, RWKV_HEADS, RWKV_HEAD_DIM, RWKV_HEAD_DIM), F32),
            jax.ShapeDtypeStruct((nc, RWKV_HEADS, RWKV_HEAD_DIM, RWKV_HEAD_DIM), F32),
            jax.ShapeDtypeStruct((t, W), F32), jax.ShapeDtypeStruct((t, W), F32),
            jax.ShapeDtypeStruct((t, W), F32), jax.ShapeDtypeStruct((t, W), F32),
        ],
        compiler_params=_params("arbitrary"),
    )(pa, pa, pa, lora, lora, lora, vec(w0), pad_rows(w_up), vec(a0), pad_rows(a_up), g_up, vec(k_k), vec(k_a))

    return pl.pallas_call(
        _rwkv_scan_body,
        grid=(nc,),
        in_specs=[
            pl.BlockSpec((1, RWKV_HEADS, RWKV_HEAD_DIM, RWKV_HEAD_DIM), lambda i: (i, 0, 0, 0)),
            pl.BlockSpec((1, RWKV_HEADS, RWKV_HEAD_DIM, RWKV_HEAD_DIM), lambda i: (i, 0, 0, 0)),
            pl.BlockSpec((L, W), row_w), pl.BlockSpec((L, W), row_w),
            pl.BlockSpec((L, W), lambda i: (i, c)),
            pl.BlockSpec((L, W), row_w),
            pl.BlockSpec((L, W), lambda i: (i, c + 2)),
            pl.BlockSpec((L, W), row_w),
            full((1, W)), full((1, W)), full((1, W)),
        ],
        out_specs=pl.BlockSpec((L, W), row_w),
        out_shape=jax.ShapeDtypeStruct((t, W), out_dtype),
        scratch_shapes=[pltpu.VMEM((RWKV_HEADS, RWKV_HEAD_DIM, RWKV_HEAD_DIM), F32)],
        compiler_params=_params("arbitrary"),
    )(m, n, ry, y0, pa, kmod, pa, g, vec(r_k), vec(ln_g), vec(ln_b))


def _mlstm_body(q_ref, k_ref, v_ref, o_ref, gates_ref, bif_ref, ng_ref, out_ref, c_ref, n_ref, m_ref):
    L, D = CHUNK, MLSTM_HEAD_DIM

    @pl.when(pl.program_id(0) == 0)
    def _():
        c_ref[...] = jnp.zeros_like(c_ref)
        n_ref[...] = jnp.zeros_like(n_ref)
        m_ref[...] = jnp.zeros_like(m_ref)

    row = lax.broadcasted_iota(jnp.int32, (L, L), 0)
    col = lax.broadcasted_iota(jnp.int32, (L, L), 1)
    incl = col <= row
    eye = (col == row).astype(F32)
    gates = gates_ref[...] + bif_ref[...]
    b_all = _dot(incl.astype(F32), jax.nn.log_sigmoid(gates), HIGHEST)
    to_row = lambda c: jnp.sum(c * eye, axis=0, keepdims=True)

    for h in range(MLSTM_HEADS):
        sl = slice(h * D, (h + 1) * D)
        q = q_ref[:, sl] * (D ** -0.5)
        k = k_ref[:, sl]
        v = v_ref[:, sl]
        i_col = gates[:, h:h + 1]
        b_col = b_all[:, MLSTM_HEADS + h:MLSTM_HEADS + h + 1]
        m_prev = m_ref[h]
        c_mat = c_ref[h]
        n_vec = n_ref[h]
        d_mat = jnp.where(incl, b_col - to_row(b_col) + to_row(i_col), -jnp.inf)
        inter = b_col + m_prev
        m_row = jnp.maximum(inter, jnp.max(d_mat, axis=-1, keepdims=True))
        w_inter = jnp.exp(inter - m_row)
        s = _dot_nt(q, k, HIGHEST) * jnp.exp(d_mat - m_row)
        num = _dot(s, v, HIGHEST) + w_inter * _dot_nt(q, c_mat, HIGHEST)
        den = jnp.sum(s, axis=-1, keepdims=True) + w_inter * jnp.sum(q * n_vec, axis=-1, keepdims=True)
        hh = num / jnp.maximum(jnp.abs(den), jnp.exp(-m_row))
        b_last = b_col[L - 1:L, :]
        d_state = b_last - b_col + i_col
        m_new = jnp.maximum(b_last + m_prev, jnp.max(d_state, axis=0, keepdims=True))
        w_state = jnp.exp(d_state - m_new)
        w_carry = jnp.exp(b_last + m_prev - m_new)
        c_ref[h] = w_carry * c_mat + _dot_tn(w_state * v, k, HIGHEST)
        n_ref[h] = w_carry * n_vec + jnp.sum(w_state * k, axis=0, keepdims=True)
        m_ref[h] = m_new
        hn = hh * lax.rsqrt(jnp.mean(hh * hh, axis=-1, keepdims=True) + NORM_EPS) * ng_ref[:, sl]
        out_ref[:, sl] = (hn * jax.nn.sigmoid(o_ref[:, sl])).astype(out_ref.dtype)


def _mlstm(qk, proj, v_blk, o_blk, gates_blk, b_if, norm_g, out_dtype=F32):
    t = qk.shape[0]
    L, W = CHUNK, MLSTM_WIDTH
    bif = jnp.pad(b_if, (0, LORA_PAD - b_if.shape[0])).reshape(1, LORA_PAD)
    return pl.pallas_call(
        _mlstm_body,
        grid=(t // L,),
        in_specs=[
            pl.BlockSpec((L, W), lambda i: (i, 0)),
            pl.BlockSpec((L, W), lambda i: (i, 1)),
            pl.BlockSpec((L, W), lambda i: (i, v_blk)),
            pl.BlockSpec((L, W), lambda i: (i, o_blk)),
            pl.BlockSpec((L, LORA_PAD), lambda i: (i, gates_blk)),
            pl.BlockSpec((1, LORA_PAD), lambda i: (0, 0)),
            pl.BlockSpec((1, W), lambda i: (0, 0)),
        ],
        out_specs=pl.BlockSpec((L, W), lambda i: (i, 0)),
        out_shape=jax.ShapeDtypeStruct((t, W), out_dtype),
        scratch_shapes=[pltpu.VMEM((MLSTM_HEADS, MLSTM_HEAD_DIM, MLSTM_HEAD_DIM), F32),
                        pltpu.VMEM((MLSTM_HEADS, 1, MLSTM_HEAD_DIM), F32),
                        pltpu.VMEM((MLSTM_HEADS, 1, 1), F32)],
        compiler_params=_params("arbitrary"),
    )(qk, qk, proj, proj, proj, bif, norm_g.reshape(1, W))


def _fox_qk_norm_body(x_ref, g_ref, o_ref):
    D = FOX_HEAD_DIM
    for h in range(FOX_HEADS):
        sl = slice(h * D, (h + 1) * D)
        x = x_ref[:, sl]
        y = x * lax.rsqrt(jnp.mean(x * x, axis=-1, keepdims=True) + NORM_EPS) * g_ref[...]
        o_ref[:, sl] = y.astype(o_ref.dtype)


def _fox_qk_norm(proj, qn_g, kn_g, tm=512):
    t = proj.shape[0]
    W = FOX_HEADS * FOX_HEAD_DIM
    gains = jnp.stack([qn_g * FOX_HEAD_DIM ** -0.5, kn_g]).reshape(2, 1, FOX_HEAD_DIM)
    return pl.pallas_call(
        _fox_qk_norm_body,
        grid=(t // tm, 2),
        in_specs=[pl.BlockSpec((tm, W), lambda i, j: (i, j)),
                  pl.BlockSpec((None, 1, FOX_HEAD_DIM), lambda i, j: (j, 0, 0))],
        out_specs=pl.BlockSpec((tm, W), lambda i, j: (i, j)),
        out_shape=jax.ShapeDtypeStruct((t, 2 * W), BF16),
        compiler_params=_params("arbitrary", "arbitrary"),
    )(proj, gains)


def _log_forget_cumsum_body(f_ref, b_ref, o_ref, carry_ref):
    tm = f_ref.shape[0]

    @pl.when(pl.program_id(0) == 0)
    def _():
        carry_ref[...] = jnp.zeros_like(carry_ref)

    row = lax.broadcasted_iota(jnp.int32, (tm, tm), 0)
    col = lax.broadcasted_iota(jnp.int32, (tm, tm), 1)
    log_f = jax.nn.log_sigmoid(f_ref[...] + b_ref[...])
    cum = _dot((col <= row).astype(F32), log_f, HIGHEST) + carry_ref[...]
    o_ref[...] = cum
    carry_ref[...] = cum[tm - 1:tm, :]


def _log_forget_cumsum(f_pre, b_f, tm=512):
    t, nh = f_pre.shape
    return pl.pallas_call(
        _log_forget_cumsum_body,
        grid=(t // tm,),
        in_specs=[pl.BlockSpec((tm, nh), lambda i: (i, 0)), pl.BlockSpec((1, nh), lambda i: (0, 0))],
        out_specs=pl.BlockSpec((tm, nh), lambda i: (i, 0)),
        out_shape=jax.ShapeDtypeStruct((t, nh), F32),
        scratch_shapes=[pltpu.VMEM((1, nh), F32)],
        compiler_params=_params("arbitrary"),
    )(f_pre, b_f.reshape(1, nh))


def _fox_body(q_ref, k_ref, v_ref, g_ref, cq_ref, ck_ref, o_ref, m_ref, l_ref, acc_ref, *, tq, tk):
    qi = pl.program_id(1)
    ki = pl.program_id(2)

    @pl.when(ki == 0)
    def _():
        m_ref[...] = jnp.full_like(m_ref, -jnp.inf)
        l_ref[...] = jnp.zeros_like(l_ref)
        acc_ref[...] = jnp.zeros_like(acc_ref)

    @pl.when(ki <= qi)
    def _():
        s = _dot_nt(q_ref[...], k_ref[...]) + cq_ref[...] - ck_ref[...]
        row = qi * tq + lax.broadcasted_iota(jnp.int32, (tq, tk), 0)
        col = ki * tk + lax.broadcasted_iota(jnp.int32, (tq, tk), 1)
        s = jnp.where(col <= row, s, -jnp.inf)
        m_prev = m_ref[...]
        m_new = jnp.maximum(m_prev, jnp.max(s, axis=-1, keepdims=True))
        p = jnp.exp(s - m_new)
        alpha = jnp.exp(m_prev - m_new)
        l_ref[...] = alpha * l_ref[...] + jnp.sum(p, axis=-1, keepdims=True)
        acc_ref[...] = alpha * acc_ref[...] + _dot(p.astype(BF16), v_ref[...].astype(BF16))
        m_ref[...] = m_new

    @pl.when(ki == qi)
    def _():
        o = acc_ref[...] / l_ref[...] * jax.nn.sigmoid(g_ref[...])
        o_ref[...] = o.astype(o_ref.dtype)


def _fox_attention(qkn, proj, v_blk0, g_blk0, cum, tq=512):
    t = qkn.shape[0]
    D, H = FOX_HEAD_DIM, FOX_HEADS
    tk = tq
    cum_t = cum.T
    cq = cum_t.reshape(H, t, 1)
    ck = cum_t.reshape(H, 1, t)
    kv = lambda h, qi, ki: jnp.minimum(ki, qi)
    return pl.pallas_call(
        functools.partial(_fox_body, tq=tq, tk=tk),
        grid=(H, t // tq, t // tk),
        in_specs=[
            pl.BlockSpec((tq, D), lambda h, qi, ki: (qi, h)),
            pl.BlockSpec((tk, D), lambda h, qi, ki: (kv(h, qi, ki), H + h)),
            pl.BlockSpec((tk, D), lambda h, qi, ki: (kv(h, qi, ki), v_blk0 + h)),
            pl.BlockSpec((tq, D), lambda h, qi, ki: (qi, g_blk0 + h)),
            pl.BlockSpec((None, tq, 1), lambda h, qi, ki: (h, qi, 0)),
            pl.BlockSpec((None, 1, tk), lambda h, qi, ki: (h, 0, kv(h, qi, ki))),
        ],
        out_specs=pl.BlockSpec((tq, D), lambda h, qi, ki: (qi, h)),
        out_shape=jax.ShapeDtypeStruct((t, H * D), BF16),
        scratch_shapes=[pltpu.VMEM((tq, 1), F32), pltpu.VMEM((tq, 1), F32), pltpu.VMEM((tq, D), F32)],
        compiler_params=_params("arbitrary", "arbitrary", "arbitrary"),
    )(qkn, qkn, proj, proj, cq, ck)


def _xattn_body(q_ref, k_ref, v_ref, o_ref):
    s = _dot_nt(q_ref[...], k_ref[...]) * (XATTN_HEAD_DIM ** -0.5)
    p = jnp.exp(s - jnp.max(s, axis=-1, keepdims=True))
    o = _dot(p.astype(BF16), v_ref[...]) / jnp.sum(p, axis=-1, keepdims=True)
    o_ref[...] = o.astype(o_ref.dtype)


def _cross_attention(q, kv, tq=512):
    t = q.shape[0]
    n_mem = kv.shape[0]
    D, H = XATTN_HEAD_DIM, XATTN_HEADS
    return pl.pallas_call(
        _xattn_body,
        grid=(t // tq, H),
        in_specs=[pl.BlockSpec((tq, D), lambda i, h: (i, h)),
                  pl.BlockSpec((n_mem, D), lambda i, h: (0, h)),
                  pl.BlockSpec((n_mem, D), lambda i, h: (0, H + h))],
        out_specs=pl.BlockSpec((tq, D), lambda i, h: (i, h)),
        out_shape=jax.ShapeDtypeStruct((t, H * D), BF16),
        compiler_params=_params("arbitrary", "arbitrary"),
    )(q, kv, kv)


def _swiglu_up_body(*refs, has_scale):
    it = iter(refs)
    x_ref, g_ref, wg_ref, wu_ref = next(it), next(it), next(it), next(it)
    scale_ref = next(it) if has_scale else None
    o_ref, xn_ref = next(it), next(it)

    @pl.when((pl.program_id(1) == 0) & (pl.program_id(2) == 0))
    def _():
        x = x_ref[...]
        xn_ref[...] = (x * lax.rsqrt(jnp.mean(x * x, axis=-1, keepdims=True) + NORM_EPS) * g_ref[...]).astype(BF16)

    xn = xn_ref[...]
    a = _dot(xn, wg_ref[...].astype(BF16))
    b = _dot(xn, wu_ref[...].astype(BF16))
    h = a * jax.nn.sigmoid(a) * b
    if has_scale:
        h = h * scale_ref[...]
    o_ref[...] = h.astype(o_ref.dtype)


def _swiglu_up(x, norm_g, w_gate, w_up, scale=None, tm=512, tf=512):
    t, d = x.shape
    ne, _, f = w_gate.shape
    nf = f // tf
    in_specs = [pl.BlockSpec((tm, d), lambda i, e, j: (i, 0)),
                pl.BlockSpec((1, d), lambda i, e, j: (0, 0)),
                pl.BlockSpec((None, d, tf), lambda i, e, j: (e, 0, j)),
                pl.BlockSpec((None, d, tf), lambda i, e, j: (e, 0, j))]
    args = [x, norm_g.reshape(1, d), w_gate, w_up]
    if scale is not None:
        in_specs.append(pl.BlockSpec((None, tm, 1), lambda i, e, j: (e, i, 0)))
        args.append(scale)
    return pl.pallas_call(
        functools.partial(_swiglu_up_body, has_scale=scale is not None),
        grid=(t // tm, ne, nf),
        in_specs=in_specs,
        out_specs=pl.BlockSpec((tm, tf), lambda i, e, j: (i, e * nf + j)),
        out_shape=jax.ShapeDtypeStruct((t, ne * f), BF16),
        scratch_shapes=[pltpu.VMEM((tm, d), BF16)],
        compiler_params=_params("arbitrary", "arbitrary", "arbitrary"),
    )(*args)


def _mm_acc_body(x_ref, w_ref, res_ref, o_ref, acc_ref):
    kk = pl.program_id(1)

    @pl.when(kk == 0)
    def _():
        acc_ref[...] = jnp.zeros_like(acc_ref)

    acc_ref[...] += _dot(x_ref[...], w_ref[...].astype(x_ref.dtype))

    @pl.when(kk == pl.num_programs(1) - 1)
    def _():
        o_ref[...] = (acc_ref[...] + res_ref[...]).astype(o_ref.dtype)


def _mm_acc(x, w, res, tm=512, tk=512):
    t, k = x.shape
    n = w.shape[1]
    return pl.pallas_call(
        _mm_acc_body,
        grid=(t // tm, k // tk),
        in_specs=[pl.BlockSpec((tm, tk), lambda i, kk: (i, kk)),
                  pl.BlockSpec((tk, n), lambda i, kk: (kk, 0)),
                  pl.BlockSpec((tm, n), lambda i, kk: (i, 0))],
        out_specs=pl.BlockSpec((tm, n), lambda i, kk: (i, 0)),
        out_shape=jax.ShapeDtypeStruct((t, n), F32),
        scratch_shapes=[pltpu.VMEM((tm, n), F32)],
        compiler_params=_params("arbitrary", "arbitrary"),
    )(x, w, res)


def _rmsnorm_body(x_ref, g_ref, o_ref):
    x = x_ref[...]
    o_ref[...] = x * lax.rsqrt(jnp.mean(x * x, axis=-1, keepdims=True) + NORM_EPS) * g_ref[...]


def _rmsnorm(x, g, tm=512):
    t, d = x.shape
    return pl.pallas_call(
        _rmsnorm_body,
        grid=(t // tm,),
        in_specs=[pl.BlockSpec((tm, d), lambda i: (i, 0)), pl.BlockSpec((1, d), lambda i: (0, 0))],
        out_specs=pl.BlockSpec((tm, d), lambda i: (i, 0)),
        out_shape=jax.ShapeDtypeStruct((t, d), F32),
        compiler_params=_params("arbitrary"),
    )(x, g.reshape(1, d))


def _pad_cols(w, n=LORA_PAD):
    return jnp.pad(w, [(0, 0)] * (w.ndim - 1) + [(0, n - w.shape[-1])])


def _pack_ab_columns(w):
    a_cols = 3 * RWKV_WIDTH + DECAY_LORA + AAA_LORA + GATE_LORA
    a, b = w[..., :a_cols], w[..., a_cols:]
    o1 = 3 * RWKV_WIDTH
    o2 = o1 + DECAY_LORA
    o3 = o2 + AAA_LORA
    return jnp.concatenate([
        a[..., :o1], b[..., :4 * MLSTM_WIDTH],
        _pad_cols(a[..., o1:o2]), _pad_cols(a[..., o2:o3]), a[..., o3:],
        _pad_cols(b[..., 4 * MLSTM_WIDTH:])], axis=-1)


def _rwkv_mlstm_mixer(x, norm_g, w_in, mu, w0, w_up, a0, a_up, g_up, k_k, k_a, r_k, ln_g, ln_b,
                      conv_qk, b_if, mh_g, w_out):
    t = x.shape[0]
    proj = _norm_mm(x, _pack_ab_columns(w_in), norm_g=norm_g)
    mu_p = _pack_ab_columns(jnp.concatenate([mu, jnp.zeros((4 * MLSTM_WIDTH + 2 * MLSTM_HEADS,), F32)]))
    o1 = 3 * RWKV_WIDTH
    o_lora = o1 + 4 * MLSTM_WIDTH

    def token_shift(p, m):
        prev = jnp.concatenate([jnp.zeros((1, p.shape[1]), p.dtype), p[:-1]], axis=0)
        return p + (prev - p) * m

    pa = token_shift(proj[:, :o1], mu_p[:o1])
    lora = token_shift(proj[:, o_lora:o_lora + 512], mu_p[o_lora:o_lora + 512])
    y_a = _rwkv7(pa, lora, w0, w_up, a0, a_up, g_up, k_k, k_a, r_k, ln_g, ln_b, out_dtype=BF16)

    qk = proj[:, o1:o1 + 2 * MLSTM_WIDTH]
    qk_pad = jnp.concatenate([jnp.zeros((MLSTM_CONV - 1, qk.shape[1]), F32), qk], axis=0)
    conv = sum(conv_qk[j] * qk_pad[j:j + t] for j in range(MLSTM_CONV))
    qk_c = conv * jax.nn.sigmoid(conv)
    y_b = _mlstm(qk_c, proj, 5, 6, (o_lora + 512) // LORA_PAD, b_if, mh_g, out_dtype=BF16)

    x = _norm_mm(y_a, w_out[:RWKV_WIDTH], res=x)
    return _norm_mm(y_b, w_out[RWKV_WIDTH:], res=x)


def _fox_layer(x, norm_g, w_in, b_f, qn_g, kn_g, w_out):
    W = FOX_HEADS * FOX_HEAD_DIM
    proj = _norm_mm(x, w_in, norm_g=norm_g)
    qkn = _fox_qk_norm(proj, qn_g, kn_g)
    cum = _log_forget_cumsum(proj[:, 4 * W:], b_f)
    o = _fox_attention(qkn, proj, 2 * FOX_HEADS, 3 * FOX_HEADS, cum)
    return _norm_mm(o, w_out, res=x)


def _xattn_layer(x, norm_g, kv, wq, wo):
    q = _norm_mm(x, wq, norm_g=norm_g, out_dtype=BF16)
    return _norm_mm(_cross_attention(q, kv), wo, res=x)


def _moe_layer(x, norm_g, w_router, w_gate, w_up, w_down):
    t = x.shape[0]
    logits = _norm_mm(x, _pad_cols(w_router), norm_g=norm_g, mxu_dtype=F32, precision=HIGHEST)[:, :N_EXPERTS]
    top_val, top_idx = lax.top_k(logits, TOP_K)
    gates = jax.nn.softmax(top_val, axis=-1)
    combine = jnp.einsum('tk,tke->te', gates, jax.nn.one_hot(top_idx, N_EXPERTS, dtype=F32))
    hidden = _swiglu_up(x, norm_g, w_gate, w_up, scale=combine.T.reshape(N_EXPERTS, t, 1))
    return _mm_acc(hidden, w_down.reshape(N_EXPERTS * D_FF, D_MODEL), x)


def kernel(x, mem, mem_norm_g, final_norm_g, norm_mix_g, norm_xattn_g, norm_ffn_g,
           xa_wq, xa_wkv, xa_wo,
           ab_w_in, ab_mu, rw_w0, rw_w_up, rw_a0, rw_a_up, rw_g_up, rw_k_k, rw_k_a, rw_r_k,
           rw_ln_g, rw_ln_b, ml_conv, ml_b_if, ml_norm_g, ab_w_out,
           ffn_w_gate, ffn_w_up, ffn_w_down,
           fox_w_in, fox_b_f, fox_qn_g, fox_kn_g, fox_w_out,
           moe_router, moe_w_gate, moe_w_up, moe_w_down):
    bsz, seq, d = x.shape
    assert bsz == 1 and d == D_MODEL and seq % 512 == 0
    xs = x[0]
    mem_s = mem[0]
    depth = norm_mix_g.shape[0]
    for layer in range(depth):
        j = layer // 2
        if layer % 2 == 0:
            xs = _rwkv_mlstm_mixer(xs, norm_mix_g[layer], ab_w_in[j], ab_mu[j], rw_w0[j], rw_w_up[j], rw_a0[j],
                                   rw_a_up[j], rw_g_up[j], rw_k_k[j], rw_k_a[j], rw_r_k[j], rw_ln_g[j],
                                   rw_ln_b[j], ml_conv[j], ml_b_if[j], ml_norm_g[j], ab_w_out[j])
        else:
            xs = _fox_layer(xs, norm_mix_g[layer], fox_w_in[j], fox_b_f[j], fox_qn_g[j], fox_kn_g[j],
                            fox_w_out[j])
        kv = _norm_mm(mem_s, xa_wkv[layer], norm_g=mem_norm_g, tm=mem_s.shape[0], out_dtype=BF16)
        xs = _xattn_layer(xs, norm_xattn_g[layer], kv, xa_wq[layer], xa_wo[layer])
        if layer % 2 == 0:
            hidden = _swiglu_up(xs, norm_ffn_g[layer], ffn_w_gate[j][None], ffn_w_up[j][None])
            xs = _mm_acc(hidden, ffn_w_down[j], xs)
        else:
            xs = _moe_layer(xs, norm_ffn_g[layer], moe_router[j], moe_w_gate[j], moe_w_up[j], moe_w_down[j])
    return _rmsnorm(xs, final_norm_g)[None]
```

```python
import functools

import jax
import jax.numpy as jnp
from jax import lax
from jax.experimental import pallas as pl
from jax.experimental.pallas import tpu as pltpu

F32 = jnp.float32
BF16 = jnp.bfloat16
HIGHEST = lax.Precision.HIGHEST

D_MODEL = 2048
CHUNK = 64
NORM_EPS = 1e-6
GN_EPS = 64e-5
RWKV_WIDTH = 1024
RWKV_HEAD_DIM = 64
RWKV_HEADS = 16
DECAY_LORA = 96
AAA_LORA = 96
GATE_LORA = 256
LORA_PAD = 128
MLSTM_WIDTH = 1024
MLSTM_HEAD_DIM = 256
MLSTM_HEADS = 4
MLSTM_CONV = 4
FOX_HEAD_DIM = 128
FOX_HEADS = 16
XATTN_HEADS = 4
XATTN_HEAD_DIM = 512
D_FF = 5632
N_EXPERTS = 8
TOP_K = 2

VMEM_LIMIT_BYTES = 56 * 1024 * 1024


def _params(*semantics):
    return pltpu.CompilerParams(dimension_semantics=semantics, vmem_limit_bytes=VMEM_LIMIT_BYTES)


def _bf(a):
    return a.astype(BF16)


def _dot(a, b, precision=None):
    return jnp.dot(a, b, preferred_element_type=F32, precision=precision)


def _dot_nt(a, b, precision=None):
    return lax.dot_general(a, b, (((1,), (1,)), ((), ())), preferred_element_type=F32, precision=precision)


def _dot_tn(a, b, precision=None):
    return lax.dot_general(a, b, (((0,), (0,)), ((), ())), preferred_element_type=F32, precision=precision)


def _norm_mm_body(*refs, has_norm, has_res, precision):
    it = iter(refs)
    x_ref = next(it)
    g_ref = next(it) if has_norm else None
    w_ref = next(it)
    res_ref = next(it) if has_res else None
    o_ref = next(it)
    xn_ref = next(it)

    @pl.when(pl.program_id(1) == 0)
    def _():
        x = x_ref[...].astype(F32)
        if has_norm:
            x = x * lax.rsqrt(jnp.mean(x * x, axis=-1, keepdims=True) + NORM_EPS) * g_ref[...]
        xn_ref[...] = x.astype(xn_ref.dtype)

    acc = _dot(xn_ref[...], w_ref[...].astype(xn_ref.dtype), precision)
    if has_res:
        acc = acc + res_ref[...]
    o_ref[...] = acc.astype(o_ref.dtype)


def _norm_mm(x, w, *, norm_g=None, res=None, tm=512, tn=512, out_dtype=F32, mxu_dtype=BF16, precision=None):
    m, k = x.shape
    n = w.shape[1]
    tm, tn = min(tm, m), min(tn, n)
    assert m % tm == 0
    in_specs = [pl.BlockSpec((tm, k), lambda i, j: (i, 0))]
    args = [x]
    if norm_g is not None:
        in_specs.append(pl.BlockSpec((1, k), lambda i, j: (0, 0)))
        args.append(norm_g.reshape(1, k))
    in_specs.append(pl.BlockSpec((k, tn), lambda i, j: (0, j)))
    args.append(w)
    if res is not None:
        in_specs.append(pl.BlockSpec((tm, tn), lambda i, j: (i, j)))
        args.append(res)
    return pl.pallas_call(
        functools.partial(_norm_mm_body, has_norm=norm_g is not None, has_res=res is not None, precision=precision),
        grid=(m // tm, pl.cdiv(n, tn)),
        in_specs=in_specs,
        out_specs=pl.BlockSpec((tm, tn), lambda i, j: (i, j)),
        out_shape=jax.ShapeDtypeStruct((m, n), out_dtype),
        scratch_shapes=[pltpu.VMEM((tm, k), mxu_dtype)],
        compiler_params=_params("arbitrary", "arbitrary"),
    )(*args)


def _rwkv_chunk_body(r_ref, k_ref, v_ref, wlo_ref, alo_ref, glo_ref,
                     w0_ref, wup_ref, a0_ref, aup_ref, gup_ref, kk_ref, ka_ref,
                     m_ref, n_ref, ry_ref, y0_ref, kmod_ref, g_ref):
    L, N = CHUNK, RWKV_HEAD_DIM
    r = r_ref[...]
    k = k_ref[...]
    v = v_ref[...]
    log_w = -jax.nn.softplus(-(w0_ref[...] + _dot(jnp.tanh(wlo_ref[...]), wup_ref[...], HIGHEST))) - 0.5
    lw = -jnp.exp(log_w)
    a = jax.nn.sigmoid(a0_ref[...] + _dot(alo_ref[...], aup_ref[...], HIGHEST))
    g_ref[...] = _dot(jax.nn.sigmoid(glo_ref[...]), gup_ref[...], HIGHEST)
    k_mod = k * (1.0 + (a - 1.0) * ka_ref[...])
    kmod_ref[...] = k_mod
    kk_raw = k * kk_ref[...]

    row = lax.broadcasted_iota(jnp.int32, (L, L), 0)
    col = lax.broadcasted_iota(jnp.int32, (L, L), 1)
    incl = col <= row
    strict = col < row
    eye = (col == row).astype(F32)
    cum = _dot(incl.astype(F32), lw, HIGHEST)
    cum_last = cum[L - 1:L, :]
    g_in = jnp.exp(cum)
    g_prev = jnp.exp(cum - lw)
    g_inv = jnp.exp(-cum)
    g_tail = jnp.exp(cum_last - cum)
    g_last = jnp.exp(cum_last)
    blk16 = (row // 16) == (col // 16)
    blk32 = (row // 32) == (col // 32)

    for h in range(RWKV_HEADS):
        sl = slice(h * N, (h + 1) * N)
        kk_h = kk_raw[:, sl]
        kk_h = kk_h / jnp.maximum(jnp.sqrt(jnp.sum(kk_h * kk_h, axis=-1, keepdims=True)), 1e-12)
        b_h = kk_h * a[:, sl]
        km_h = k_mod[:, sl]
        v_h = v[:, sl]
        rg = r[:, sl] * g_in[:, sl]
        kkg = _bf(kk_h * g_prev[:, sl])
        rg_b = _bf(rg)
        kd = _bf(km_h * g_inv[:, sl])
        bd = _bf(b_h * g_inv[:, sl])
        kdg = _bf(km_h * g_tail[:, sl])
        bdg = _bf(b_h * g_tail[:, sl])
        v_b = _bf(v_h)

        a_k = jnp.where(strict, _dot_nt(kkg, kd), 0.0)
        a_b = jnp.where(strict, _dot_nt(kkg, bd), 0.0)
        ar_k = _bf(jnp.where(incl, _dot_nt(rg_b, kd), 0.0))
        ar_b = _bf(jnp.where(incl, _dot_nt(rg_b, bd), 0.0))

        neg = jnp.where(blk16, -a_b, 0.0)
        t_inv = eye + neg
        pw = neg
        for _ in range(3):
            pw_b = _bf(pw)
            pw = _dot(pw_b, pw_b)
            t_inv = t_inv + _dot(_bf(t_inv), _bf(pw))
        off1 = _bf(jnp.where(blk32 & (~blk16), a_b, 0.0))
        t_b = _bf(t_inv)
        t_inv = t_inv - _dot(_bf(_dot(t_b, off1)), t_b)
        off2 = _bf(jnp.where(blk32, 0.0, a_b))
        t_b = _bf(t_inv)
        t_inv = t_inv - _dot(_bf(_dot(t_b, off2)), t_b)
        t_b = _bf(t_inv)

        p = _bf(_dot(t_b, kkg))
        q = _bf(_dot(t_b, _bf(_dot(_bf(a_k), v_b))))
        ry_ref[:, sl] = rg - _dot(ar_b, p)
        y0_ref[:, sl] = _dot(ar_k, v_b) - _dot(ar_b, q)
        m_ref[0, h] = eye * g_last[:, sl] - _dot_tn(bdg, p)
        n_ref[0, h] = _dot_tn(kdg, v_b) - _dot_tn(bdg, q)


def _rwkv_scan_body(m_ref, n_ref, ry_ref, y0_ref, r_ref, kmod_ref, v_ref, g_ref, rk_ref, lng_ref, lnb_ref,
                    o_ref, h_ref):
    N = RWKV_HEAD_DIM

    @pl.when(pl.program_id(0) == 0)
    def _():
        h_ref[...] = jnp.zeros_like(h_ref)

    for h in range(RWKV_HEADS):
        sl = slice(h * N, (h + 1) * N)
        state = _bf(h_ref[h])
        y = _dot(_bf(ry_ref[:, sl]), state) + y0_ref[:, sl]
        h_ref[h] = _dot(_bf(m_ref[0, h]), state) + n_ref[0, h]
        mean = jnp.mean(y, axis=-1, keepdims=True)
        var = jnp.mean(jnp.square(y - mean), axis=-1, keepdims=True)
        y = (y - mean) * lax.rsqrt(var + GN_EPS) * lng_ref[:, sl] + lnb_ref[:, sl]
        v_h = v_ref[:, sl]
        bonus = jnp.sum(r_ref[:, sl] * kmod_ref[:, sl] * rk_ref[:, sl], axis=-1, keepdims=True) * v_h
        o_ref[:, sl] = ((y + bonus) * g_ref[:, sl]).astype(o_ref.dtype)


def _rwkv7(pa, lora, w0, w_up, a0, a_up, g_up, k_k, k_a, r_k, ln_g, ln_b, out_dtype=F32):
    t = pa.shape[0]
    nc = t // CHUNK
    W, L = RWKV_WIDTH, CHUNK
    c = 0
    row_w = lambda i: (i, 0)
    vec = lambda x: x.reshape(1, W)
    pad_rows = lambda x: jnp.pad(x, ((0, LORA_PAD - x.shape[0]), (0, 0)))
    full = lambda shape: pl.BlockSpec(shape, lambda i: (0,) * len(shape))
    m, n, ry, y0, kmod, g = pl.pallas_call(
        _rwkv_chunk_body,
        grid=(nc,),
        in_specs=[
            pl.BlockSpec((L, W), lambda i: (i, c)),
            pl.BlockSpec((L, W), lambda i: (i, c + 1)),
            pl.BlockSpec((L, W), lambda i: (i, c + 2)),
            pl.BlockSpec((L, LORA_PAD), lambda i: (i, 0)),
            pl.BlockSpec((L, LORA_PAD), lambda i: (i, 1)),
            pl.BlockSpec((L, GATE_LORA), lambda i: (i, 1)),
            full((1, W)), full((LORA_PAD, W)), full((1, W)), full((LORA_PAD, W)), full((GATE_LORA, W)),
            full((1, W)), full((1, W)),
        ],
        out_specs=[
            pl.BlockSpec((1, RWKV_HEADS, RWKV_HEAD_DIM, RWKV_HEAD_DIM), lambda i: (i, 0, 0, 0)),
            pl.BlockSpec((1, RWKV_HEADS, RWKV_HEAD_DIM, RWKV_HEAD_DIM), lambda i: (i, 0, 0, 0)),
            pl.BlockSpec((L, W), row_w), pl.BlockSpec((L, W), row_w),
            pl.BlockSpec((L, W), row_w), pl.BlockSpec((L, W), row_w),
        ],
        out_shape=[
            jax.ShapeDtypeStruct((nc, RWKV_HEADS, RWKV_HEAD_DIM, RWKV_HEAD_DIM), F32),
            jax.ShapeDtypeStruct((nc, RWKV_HEADS, RWKV_HEAD_DIM, RWKV_HEAD_DIM), F32),
            jax.ShapeDtypeStruct((t, W), F32), jax.ShapeDtypeStruct((t, W), F32),
            jax.ShapeDtypeStruct((t, W), F32), jax.ShapeDtypeStruct((t, W), F32),
        ],
        compiler_params=_params("arbitrary"),
    )(pa, pa, pa, lora, lora, lora, vec(w0), pad_rows(w_up), vec(a0), pad_rows(a_up), g_up, vec(k_k), vec(k_a))

    return pl.pallas_call(
        _rwkv_scan_body,
        grid=(nc,),
        in_specs=[
            pl.BlockSpec((1, RWKV_HEADS, RWKV_HEAD_DIM, RWKV_HEAD_DIM), lambda i: (i, 0, 0, 0)),
            pl.BlockSpec((1, RWKV_HEADS, RWKV_HEAD_DIM, RWKV_HEAD_DIM), lambda i: (i, 0, 0, 0)),
            pl.BlockSpec((L, W), row_w), pl.BlockSpec((L, W), row_w),
            pl.BlockSpec((L, W), lambda i: (i, c)),
            pl.BlockSpec((L, W), row_w),
            pl.BlockSpec((L, W), lambda i: (i, c + 2)),
            pl.BlockSpec((L, W), row_w),
            full((1, W)), full((1, W)), full((1, W)),
        ],
        out_specs=pl.BlockSpec((L, W), row_w),
        out_shape=jax.ShapeDtypeStruct((t, W), out_dtype),
        scratch_shapes=[pltpu.VMEM((RWKV_HEADS, RWKV_HEAD_DIM, RWKV_HEAD_DIM), F32)],
        compiler_params=_params("arbitrary"),
    )(m, n, ry, y0, pa, kmod, pa, g, vec(r_k), vec(ln_g), vec(ln_b))


def _mlstm_body(q_ref, k_ref, v_ref, o_ref, gates_ref, bif_ref, ng_ref, out_ref, c_ref, n_ref, m_ref):
    L, D = CHUNK, MLSTM_HEAD_DIM

    @pl.when(pl.program_id(0) == 0)
    def _():
        c_ref[...] = jnp.zeros_like(c_ref)
        n_ref[...] = jnp.zeros_like(n_ref)
        m_ref[...] = jnp.zeros_like(m_ref)

    row = lax.broadcasted_iota(jnp.int32, (L, L), 0)
    col = lax.broadcasted_iota(jnp.int32, (L, L), 1)
    incl = col <= row
    eye = (col == row).astype(F32)
    gates = gates_ref[...] + bif_ref[...]
    b_all = _dot(incl.astype(F32), jax.nn.log_sigmoid(gates), HIGHEST)
    to_row = lambda c: jnp.sum(c * eye, axis=0, keepdims=True)

    for h in range(MLSTM_HEADS):
        sl = slice(h * D, (h + 1) * D)
        q = q_ref[:, sl] * (D ** -0.5)
        k = k_ref[:, sl]
        v = v_ref[:, sl]
        i_col = gates[:, h:h + 1]
        b_col = b_all[:, MLSTM_HEADS + h:MLSTM_HEADS + h + 1]
        m_prev = m_ref[h]
        c_mat = c_ref[h]
        n_vec = n_ref[h]
        d_mat = jnp.where(incl, b_col - to_row(b_col) + to_row(i_col), -jnp.inf)
        inter = b_col + m_prev
        m_row = jnp.maximum(inter, jnp.max(d_mat, axis=-1, keepdims=True))
        w_inter = jnp.exp(inter - m_row)
        q_b, k_b = _bf(q), _bf(k)
        s = _dot_nt(q_b, k_b) * jnp.exp(d_mat - m_row)
        num = _dot(_bf(s), _bf(v)) + w_inter * _dot_nt(q_b, _bf(c_mat))
        den = jnp.sum(s, axis=-1, keepdims=True) + w_inter * jnp.sum(q * n_vec, axis=-1, keepdims=True)
        hh = num / jnp.maximum(jnp.abs(den), jnp.exp(-m_row))
        b_last = b_col[L - 1:L, :]
        d_state = b_last - b_col + i_col
        m_new = jnp.maximum(b_last + m_prev, jnp.max(d_state, axis=0, keepdims=True))
        w_state = jnp.exp(d_state - m_new)
        w_carry = jnp.exp(b_last + m_prev - m_new)
        c_ref[h] = w_carry * c_mat + _dot_tn(_bf(w_state * v), k_b)
        n_ref[h] = w_carry * n_vec + jnp.sum(w_state * k, axis=0, keepdims=True)
        m_ref[h] = m_new
        hn = hh * lax.rsqrt(jnp.mean(hh * hh, axis=-1, keepdims=True) + NORM_EPS) * ng_ref[:, sl]
        out_ref[:, sl] = (hn * jax.nn.sigmoid(o_ref[:, sl])).astype(out_ref.dtype)


def _mlstm(qk, proj, v_blk, o_blk, gates_blk, b_if, norm_g, out_dtype=F32):
    t = qk.shape[0]
    L, W = CHUNK, MLSTM_WIDTH
    bif = jnp.pad(b_if, (0, LORA_PAD - b_if.shape[0])).reshape(1, LORA_PAD)
    return pl.pallas_call(
        _mlstm_body,
        grid=(t // L,),
        in_specs=[
            pl.BlockSpec((L, W), lambda i: (i, 0)),
            pl.BlockSpec((L, W), lambda i: (i, 1)),
            pl.BlockSpec((L, W), lambda i: (i, v_blk)),
            pl.BlockSpec((L, W), lambda i: (i, o_blk)),
            pl.BlockSpec((L, LORA_PAD), lambda i: (i, gates_blk)),
            pl.BlockSpec((1, LORA_PAD), lambda i: (0, 0)),
            pl.BlockSpec((1, W), lambda i: (0, 0)),
        ],
        out_specs=pl.BlockSpec((L, W), lambda i: (i, 0)),
        out_shape=jax.ShapeDtypeStruct((t, W), out_dtype),
        scratch_shapes=[pltpu.VMEM((MLSTM_HEADS, MLSTM_HEAD_DIM, MLSTM_HEAD_DIM), F32),
                        pltpu.VMEM((MLSTM_HEADS, 1, MLSTM_HEAD_DIM), F32),
                        pltpu.VMEM((MLSTM_HEADS, 1, 1), F32)],
        compiler_params=_params("arbitrary"),
    )(qk, qk, proj, proj, proj, bif, norm_g.reshape(1, W))


def _fox_qkv_prep_body(x_ref, g_ref, o_ref):
    D = FOX_HEAD_DIM

    @pl.when(pl.program_id(1) < 2)
    def _():
        for h in range(FOX_HEADS):
            sl = slice(h * D, (h + 1) * D)
            x = x_ref[:, sl]
            y = x * lax.rsqrt(jnp.mean(x * x, axis=-1, keepdims=True) + NORM_EPS) * g_ref[...]
            o_ref[:, sl] = y.astype(o_ref.dtype)

    @pl.when(pl.program_id(1) == 2)
    def _():
        o_ref[...] = x_ref[...].astype(o_ref.dtype)


def _fox_qkv_prep(proj, qn_g, kn_g, tm=512):
    t = proj.shape[0]
    W = FOX_HEADS * FOX_HEAD_DIM
    gains = jnp.stack([qn_g * FOX_HEAD_DIM ** -0.5, kn_g, jnp.ones_like(kn_g)]).reshape(3, 1, FOX_HEAD_DIM)
    return pl.pallas_call(
        _fox_qkv_prep_body,
        grid=(t // tm, 3),
        in_specs=[pl.BlockSpec((tm, W), lambda i, j: (i, j)),
                  pl.BlockSpec((None, 1, FOX_HEAD_DIM), lambda i, j: (j, 0, 0))],
        out_specs=pl.BlockSpec((tm, W), lambda i, j: (i, j)),
        out_shape=jax.ShapeDtypeStruct((t, 3 * W), BF16),
        compiler_params=_params("arbitrary", "arbitrary"),
    )(proj, gains)


def _log_forget_cumsum_body(f_ref, b_ref, o_ref, carry_ref):
    tm = f_ref.shape[0]

    @pl.when(pl.program_id(0) == 0)
    def _():
        carry_ref[...] = jnp.zeros_like(carry_ref)

    row = lax.broadcasted_iota(jnp.int32, (tm, tm), 0)
    col = lax.broadcasted_iota(jnp.int32, (tm, tm), 1)
    log_f = jax.nn.log_sigmoid(f_ref[...] + b_ref[...])
    cum = _dot((col <= row).astype(F32), log_f, HIGHEST) + carry_ref[...]
    o_ref[...] = cum
    carry_ref[...] = cum[tm - 1:tm, :]


def _log_forget_cumsum(f_pre, b_f, tm=512):
    t, nh = f_pre.shape
    return pl.pallas_call(
        _log_forget_cumsum_body,
        grid=(t // tm,),
        in_specs=[pl.BlockSpec((tm, nh), lambda i: (i, 0)), pl.BlockSpec((1, nh), lambda i: (0, 0))],
        out_specs=pl.BlockSpec((tm, nh), lambda i: (i, 0)),
        out_shape=jax.ShapeDtypeStruct((t, nh), F32),
        scratch_shapes=[pltpu.VMEM((1, nh), F32)],
        compiler_params=_params("arbitrary"),
    )(f_pre, b_f.reshape(1, nh))


def _fox_body(q_ref, k_ref, v_ref, g_ref, cq_ref, ck_ref, o_ref, *, tq, tk):
    qi = pl.program_id(1)
    q = q_ref[...]
    cq = cq_ref[...]

    def block(kb, carry, diagonal):
        m_prev, l_prev, acc = carry
        start = pl.multiple_of(kb * tk, tk)
        s = _dot_nt(q, k_ref[pl.ds(start, tk), :]) + cq - ck_ref[kb]
        if diagonal:
            row = lax.broadcasted_iota(jnp.int32, (tq, tk), 0)
            col = lax.broadcasted_iota(jnp.int32, (tq, tk), 1)
            s = jnp.where(col <= row, s, -jnp.inf)
        m_new = jnp.maximum(m_prev, jnp.max(s, axis=-1, keepdims=True))
        p = jnp.exp(s - m_new)
        alpha = jnp.exp(m_prev - m_new)
        l_new = alpha * l_prev + jnp.sum(p, axis=-1, keepdims=True)
        acc = alpha * acc + _dot(_bf(p), v_ref[pl.ds(start, tk), :])
        return m_new, l_new, acc

    init = (jnp.full((tq, 1), -jnp.inf, F32), jnp.zeros((tq, 1), F32), jnp.zeros((tq, FOX_HEAD_DIM), F32))
    carry = lax.fori_loop(0, qi, lambda kb, c: block(kb, c, False), init)
    _, l_fin, acc = block(qi, carry, True)
    o_ref[...] = (acc / l_fin * jax.nn.sigmoid(g_ref[...])).astype(o_ref.dtype)


def _fox_attention(qkv, proj, g_blk0, cum, tq=512):
    t = qkv.shape[0]
    D, H = FOX_HEAD_DIM, FOX_HEADS
    tk = tq
    cum_t = cum.T
    cq = cum_t.reshape(H, t, 1)
    ck = cum_t.reshape(H, t // tk, 1, tk)
    return pl.pallas_call(
        functools.partial(_fox_body, tq=tq, tk=tk),
        grid=(H, t // tq),
        in_specs=[
            pl.BlockSpec((tq, D), lambda h, qi: (qi, h)),
            pl.BlockSpec((t, D), lambda h, qi: (0, H + h)),
            pl.BlockSpec((t, D), lambda h, qi: (0, 2 * H + h)),
            pl.BlockSpec((tq, D), lambda h, qi: (qi, g_blk0 + h)),
            pl.BlockSpec((None, tq, 1), lambda h, qi: (h, qi, 0)),
            pl.BlockSpec((None, t // tk, 1, tk), lambda h, qi: (h, 0, 0, 0)),
        ],
        out_specs=pl.BlockSpec((tq, D), lambda h, qi: (qi, h)),
        out_shape=jax.ShapeDtypeStruct((t, H * D), BF16),
        compiler_params=_params("arbitrary", "arbitrary"),
    )(qkv, qkv, qkv, proj, cq, ck)


def _xattn_body(q_ref, k_ref, v_ref, o_ref):
    s = _dot_nt(q_ref[...], k_ref[...]) * (XATTN_HEAD_DIM ** -0.5)
    p = jnp.exp(s - jnp.max(s, axis=-1, keepdims=True))
    o = _dot(p.astype(BF16), v_ref[...]) / jnp.sum(p, axis=-1, keepdims=True)
    o_ref[...] = o.astype(o_ref.dtype)


def _cross_attention(q, kv, tq=512):
    t = q.shape[0]
    n_mem = kv.shape[0]
    D, H = XATTN_HEAD_DIM, XATTN_HEADS
    return pl.pallas_call(
        _xattn_body,
        grid=(t // tq, H),
        in_specs=[pl.BlockSpec((tq, D), lambda i, h: (i, h)),
                  pl.BlockSpec((n_mem, D), lambda i, h: (0, h)),
                  pl.BlockSpec((n_mem, D), lambda i, h: (0, H + h))],
        out_specs=pl.BlockSpec((tq, D), lambda i, h: (i, h)),
        out_shape=jax.ShapeDtypeStruct((t, H * D), BF16),
        compiler_params=_params("arbitrary", "arbitrary"),
    )(q, kv, kv)


def _swiglu_up_body(*refs, has_scale):
    it = iter(refs)
    x_ref, g_ref, wg_ref, wu_ref = next(it), next(it), next(it), next(it)
    scale_ref = next(it) if has_scale else None
    o_ref, xn_ref = next(it), next(it)

    @pl.when((pl.program_id(1) == 0) & (pl.program_id(2) == 0))
    def _():
        x = x_ref[...]
        xn_ref[...] = (x * lax.rsqrt(jnp.mean(x * x, axis=-1, keepdims=True) + NORM_EPS) * g_ref[...]).astype(BF16)

    xn = xn_ref[...]
    a = _dot(xn, wg_ref[...].astype(BF16))
    b = _dot(xn, wu_ref[...].astype(BF16))
    h = a * jax.nn.sigmoid(a) * b
    if has_scale:
        h = h * scale_ref[...]
    o_ref[...] = h.astype(o_ref.dtype)


def _swiglu_up(x, norm_g, w_gate, w_up, scale=None, tm=512, tf=512):
    t, d = x.shape
    ne, _, f = w_gate.shape
    nf = f // tf
    in_specs = [pl.BlockSpec((tm, d), lambda i, e, j: (i, 0)),
                pl.BlockSpec((1, d), lambda i, e, j: (0, 0)),
                pl.BlockSpec((None, d, tf), lambda i, e, j: (e, 0, j)),
                pl.BlockSpec((None, d, tf), lambda i, e, j: (e, 0, j))]
    args = [x, norm_g.reshape(1, d), w_gate, w_up]
    if scale is not None:
        in_specs.append(pl.BlockSpec((None, tm, 1), lambda i, e, j: (e, i, 0)))
        args.append(scale)
    return pl.pallas_call(
        functools.partial(_swiglu_up_body, has_scale=scale is not None),
        grid=(t // tm, ne, nf),
        in_specs=in_specs,
        out_specs=pl.BlockSpec((tm, tf), lambda i, e, j: (i, e * nf + j)),
        out_shape=jax.ShapeDtypeStruct((t, ne * f), BF16),
        scratch_shapes=[pltpu.VMEM((tm, d), BF16)],
        compiler_params=_params("arbitrary", "arbitrary", "arbitrary"),
    )(*args)


def _mm_acc_body(x_ref, w_ref, res_ref, o_ref, acc_ref):
    kk = pl.program_id(1)

    @pl.when(kk == 0)
    def _():
        acc_ref[...] = jnp.zeros_like(acc_ref)

    acc_ref[...] += _dot(x_ref[...], w_ref[...].astype(x_ref.dtype))

    @pl.when(kk == pl.num_programs(1) - 1)
    def _():
        o_ref[...] = (acc_ref[...] + res_ref[...]).astype(o_ref.dtype)


def _mm_acc(x, w, res, tm=512, tk=512):
    t, k = x.shape
    n = w.shape[1]
    return pl.pallas_call(
        _mm_acc_body,
        grid=(t // tm, k // tk),
        in_specs=[pl.BlockSpec((tm, tk), lambda i, kk: (i, kk)),
                  pl.BlockSpec((tk, n), lambda i, kk: (kk, 0)),
                  pl.BlockSpec((tm, n), lambda i, kk: (i, 0))],
        out_specs=pl.BlockSpec((tm, n), lambda i, kk: (i, 0)),
        out_shape=jax.ShapeDtypeStruct((t, n), F32),
        scratch_shapes=[pltpu.VMEM((tm, n), F32)],
        compiler_params=_params("arbitrary", "arbitrary"),
    )(x, w, res)


def _rmsnorm_body(x_ref, g_ref, o_ref):
    x = x_ref[...]
    o_ref[...] = x * lax.rsqrt(jnp.mean(x * x, axis=-1, keepdims=True) + NORM_EPS) * g_ref[...]


def _rmsnorm(x, g, tm=512):
    t, d = x.shape
    return pl.pallas_call(
        _rmsnorm_body,
        grid=(t // tm,),
        in_specs=[pl.BlockSpec((tm, d), lambda i: (i, 0)), pl.BlockSpec((1, d), lambda i: (0, 0))],
        out_specs=pl.BlockSpec((tm, d), lambda i: (i, 0)),
        out_shape=jax.ShapeDtypeStruct((t, d), F32),
        compiler_params=_params("arbitrary"),
    )(x, g.reshape(1, d))


def _pad_cols(w, n=LORA_PAD):
    return jnp.pad(w, [(0, 0)] * (w.ndim - 1) + [(0, n - w.shape[-1])])


def _pack_ab_columns(w):
    a_cols = 3 * RWKV_WIDTH + DECAY_LORA + AAA_LORA + GATE_LORA
    a, b = w[..., :a_cols], w[..., a_cols:]
    o1 = 3 * RWKV_WIDTH
    o2 = o1 + DECAY_LORA
    o3 = o2 + AAA_LORA
    return jnp.concatenate([
        a[..., :o1], b[..., :4 * MLSTM_WIDTH],
        _pad_cols(a[..., o1:o2]), _pad_cols(a[..., o2:o3]), a[..., o3:],
        _pad_cols(b[..., 4 * MLSTM_WIDTH:])], axis=-1)


def _rwkv_mlstm_mixer(x, norm_g, w_in, mu, w0, w_up, a0, a_up, g_up, k_k, k_a, r_k, ln_g, ln_b,
                      conv_qk, b_if, mh_g, w_out):
    t = x.shape[0]
    proj = _norm_mm(x, _pack_ab_columns(w_in), norm_g=norm_g)
    mu_p = _pack_ab_columns(jnp.concatenate([mu, jnp.zeros((4 * MLSTM_WIDTH + 2 * MLSTM_HEADS,), F32)]))
    o1 = 3 * RWKV_WIDTH
    o_lora = o1 + 4 * MLSTM_WIDTH

    def token_shift(p, m):
        prev = jnp.concatenate([jnp.zeros((1, p.shape[1]), p.dtype), p[:-1]], axis=0)
        return p + (prev - p) * m

    pa = token_shift(proj[:, :o1], mu_p[:o1])
    lora = token_shift(proj[:, o_lora:o_lora + 512], mu_p[o_lora:o_lora + 512])
    y_a = _rwkv7(pa, lora, w0, w_up, a0, a_up, g_up, k_k, k_a, r_k, ln_g, ln_b, out_dtype=BF16)

    qk = proj[:, o1:o1 + 2 * MLSTM_WIDTH]
    qk_pad = jnp.concatenate([jnp.zeros((MLSTM_CONV - 1, qk.shape[1]), F32), qk], axis=0)
    conv = sum(conv_qk[j] * qk_pad[j:j + t] for j in range(MLSTM_CONV))
    qk_c = conv * jax.nn.sigmoid(conv)
    y_b = _mlstm(qk_c, proj, 5, 6, (o_lora + 512) // LORA_PAD, b_if, mh_g, out_dtype=BF16)

    x = _norm_mm(y_a, w_out[:RWKV_WIDTH], res=x)
    return _norm_mm(y_b, w_out[RWKV_WIDTH:], res=x)


def _fox_layer(x, norm_g, w_in, b_f, qn_g, kn_g, w_out):
    W = FOX_HEADS * FOX_HEAD_DIM
    proj = _norm_mm(x, w_in, norm_g=norm_g)
    qkv = _fox_qkv_prep(proj, qn_g, kn_g)
    cum = _log_forget_cumsum(proj[:, 4 * W:], b_f)
    o = _fox_attention(qkv, proj, 3 * FOX_HEADS, cum)
    return _norm_mm(o, w_out, res=x)


def _xattn_layer(x, norm_g, kv, wq, wo):
    q = _norm_mm(x, wq, norm_g=norm_g, out_dtype=BF16)
    return _norm_mm(_cross_attention(q, kv), wo, res=x)


def _router_body(x_ref, g_ref, w_ref, logits_ref, xn_ref):
    x = x_ref[...]
    xn = x * lax.rsqrt(jnp.mean(x * x, axis=-1, keepdims=True) + NORM_EPS) * g_ref[...]
    logits_ref[...] = _dot(xn, w_ref[...], HIGHEST)
    xn_ref[...] = xn.astype(xn_ref.dtype)


def _router(x, norm_g, w_router, tm=512):
    t, d = x.shape
    return pl.pallas_call(
        _router_body,
        grid=(t // tm,),
        in_specs=[pl.BlockSpec((tm, d), lambda i: (i, 0)), pl.BlockSpec((1, d), lambda i: (0, 0)),
                  pl.BlockSpec((d, LORA_PAD), lambda i: (0, 0))],
        out_specs=[pl.BlockSpec((tm, LORA_PAD), lambda i: (i, 0)), pl.BlockSpec((tm, d), lambda i: (i, 0))],
        out_shape=[jax.ShapeDtypeStruct((t, LORA_PAD), F32), jax.ShapeDtypeStruct((t, d), BF16)],
        compiler_params=_params("arbitrary"),
    )(x, norm_g.reshape(1, d), _pad_cols(w_router))


def _moe_experts_body(tile_e_ref, n_valid_ref, x_ref, wg_ref, wu_ref, wd_ref, gate_ref, o_ref):
    i = pl.program_id(0)
    j = pl.program_id(1)

    @pl.when(j == 0)
    def _():
        o_ref[...] = jnp.zeros_like(o_ref)

    @pl.when(i < n_valid_ref[0])
    def _():
        x = x_ref[...]
        a = _dot(x, _bf(wg_ref[...]))
        b = _dot(x, _bf(wu_ref[...]))
        h = a * jax.nn.sigmoid(a) * b
        o_ref[...] += _dot(_bf(h), _bf(wd_ref[...]))

    @pl.when(j == pl.num_programs(1) - 1)
    def _():
        o_ref[...] = o_ref[...] * gate_ref[...]


def _moe_experts(xg, gate_at, tile_e, n_valid, w_gate, w_up, w_down, tm, tf=256):
    p, d = xg.shape
    f = w_gate.shape[2]
    nf = f // tf

    def expert(i, te, nv):
        return te[jnp.minimum(i, nv[0] - 1)]

    def fblock(i, j, nv):
        return jnp.where(i < nv[0], j, nf - 1)

    grid_spec = pltpu.PrefetchScalarGridSpec(
        num_scalar_prefetch=2,
        grid=(p // tm, nf),
        in_specs=[
            pl.BlockSpec((tm, d), lambda i, j, te, nv: (i, 0)),
            pl.BlockSpec((None, d, tf), lambda i, j, te, nv: (expert(i, te, nv), 0, fblock(i, j, nv))),
            pl.BlockSpec((None, d, tf), lambda i, j, te, nv: (expert(i, te, nv), 0, fblock(i, j, nv))),
            pl.BlockSpec((None, tf, d), lambda i, j, te, nv: (expert(i, te, nv), fblock(i, j, nv), 0)),
            pl.BlockSpec((tm, 1), lambda i, j, te, nv: (i, 0)),
        ],
        out_specs=pl.BlockSpec((tm, d), lambda i, j, te, nv: (i, 0)),
    )
    return pl.pallas_call(
        _moe_experts_body,
        grid_spec=grid_spec,
        out_shape=jax.ShapeDtypeStruct((p, d), F32),
        compiler_params=_params("arbitrary", "arbitrary"),
    )(tile_e, n_valid, xg, w_gate, w_up, w_down, gate_at)


MOE_TILE = 1024


def _moe_layer(x, norm_g, w_router, w_gate, w_up, w_down):
    t = x.shape[0]
    tm = MOE_TILE
    logits, xn = _router(x, norm_g, w_router)
    top_val, top_idx = lax.top_k(logits[:, :N_EXPERTS], TOP_K)
    gates = jax.nn.softmax(top_val, axis=-1)
    flat_e = top_idx.reshape(-1)
    onehot = (flat_e[:, None] == jnp.arange(N_EXPERTS)[None, :]).astype(jnp.int32)
    ranks = jnp.cumsum(onehot, axis=0) - onehot
    counts = jnp.sum(onehot, axis=0)
    padded = (counts + tm - 1) // tm * tm
    pad_end = jnp.cumsum(padded)
    dest = (pad_end - padded)[flat_e] + jnp.sum(ranks * onehot, axis=1)
    n_rows = TOP_K * t + N_EXPERTS * tm
    tok_at = jnp.zeros((n_rows,), jnp.int32).at[dest].set(jnp.arange(TOP_K * t, dtype=jnp.int32) // TOP_K)
    gate_at = jnp.zeros((n_rows,), F32).at[dest].set(gates.reshape(-1))
    tile_start = jnp.arange(n_rows // tm, dtype=jnp.int32) * tm
    tile_e = jnp.minimum(jnp.sum((tile_start[:, None] >= pad_end[None, :]).astype(jnp.int32), axis=1), N_EXPERTS - 1)
    n_valid = (pad_end[-1:] // tm).astype(jnp.int32)
    xg = jnp.take(xn, tok_at, axis=0)
    y = _moe_experts(xg, gate_at.reshape(n_rows, 1), tile_e, n_valid, w_gate, w_up, w_down, tm)
    dest2 = dest.reshape(t, TOP_K)
    return x + jnp.take(y, dest2[:, 0], axis=0) + jnp.take(y, dest2[:, 1], axis=0)


def kernel(x, mem, mem_norm_g, final_norm_g, norm_mix_g, norm_xattn_g, norm_ffn_g,
           xa_wq, xa_wkv, xa_wo,
           ab_w_in, ab_mu, rw_w0, rw_w_up, rw_a0, rw_a_up, rw_g_up, rw_k_k, rw_k_a, rw_r_k,
           rw_ln_g, rw_ln_b, ml_conv, ml_b_if, ml_norm_g, ab_w_out,
           ffn_w_gate, ffn_w_up, ffn_w_down,
           fox_w_in, fox_b_f, fox_qn_g, fox_kn_g, fox_w_out,
           moe_router, moe_w_gate, moe_w_up, moe_w_down):
    bsz, seq, d = x.shape
    assert bsz == 1 and d == D_MODEL and seq % 512 == 0
    xs = x[0]
    mem_s = mem[0]
    depth = norm_mix_g.shape[0]
    for layer in range(depth):
        j = layer // 2
        if layer % 2 == 0:
            xs = _rwkv_mlstm_mixer(xs, norm_mix_g[layer], ab_w_in[j], ab_mu[j], rw_w0[j], rw_w_up[j], rw_a0[j],
                                   rw_a_up[j], rw_g_up[j], rw_k_k[j], rw_k_a[j], rw_r_k[j], rw_ln_g[j],
                                   rw_ln_b[j], ml_conv[j], ml_b_if[j], ml_norm_g[j], ab_w_out[j])
        else:
            xs = _fox_layer(xs, norm_mix_g[layer], fox_w_in[j], fox_b_f[j], fox_qn_g[j], fox_kn_g[j],
                            fox_w_out[j])
        kv = _norm_mm(mem_s, xa_wkv[layer], norm_g=mem_norm_g, tm=mem_s.shape[0], out_dtype=BF16)
        xs = _xattn_layer(xs, norm_xattn_g[layer], kv, xa_wq[layer], xa_wo[layer])
        if layer % 2 == 0:
            hidden = _swiglu_up(xs, norm_ffn_g[layer], ffn_w_gate[j][None], ffn_w_up[j][None])
            xs = _mm_acc(hidden, ffn_w_down[j], xs)
        else:
            xs = _moe_layer(xs, norm_ffn_g[layer], moe_router[j], moe_w_gate[j], moe_w_up[j], moe_w_down[j])
    return _rmsnorm(xs, final_norm_g)[None]
```

```python
import functools

import jax
import jax.numpy as jnp
from jax import lax
from jax.experimental import pallas as pl
from jax.experimental.pallas import tpu as pltpu

F32 = jnp.float32
BF16 = jnp.bfloat16
HIGHEST = lax.Precision.HIGHEST

D_MODEL = 2048
CHUNK = 64
NORM_EPS = 1e-6
GN_EPS = 64e-5
RWKV_WIDTH = 1024
RWKV_HEAD_DIM = 64
RWKV_HEADS = 16
RWKV_GROUP = 256
DECAY_LORA = 96
AAA_LORA = 96
GATE_LORA = 256
LORA_PAD = 128
MLSTM_WIDTH = 1024
MLSTM_HEAD_DIM = 256
MLSTM_HEADS = 4
MLSTM_CONV = 4
FOX_HEAD_DIM = 128
FOX_HEADS = 16
XATTN_HEADS = 4
XATTN_HEAD_DIM = 512
D_FF = 5632
N_EXPERTS = 8
TOP_K = 2

VMEM_LIMIT_BYTES = 56 * 1024 * 1024
LOG2E = 1.4426950408889634
FOX_SUB_TILE = 128
MM_ROW_TILE = 1024
MM_COL_TILE = 512


def _params(*semantics):
    return pltpu.CompilerParams(dimension_semantics=semantics, vmem_limit_bytes=VMEM_LIMIT_BYTES)


def _bf(a):
    return a.astype(BF16)


def _dot(a, b, precision=None):
    return jnp.dot(a, b, preferred_element_type=F32, precision=precision)


def _dot_nt(a, b, precision=None):
    return lax.dot_general(a, b, (((1,), (1,)), ((), ())), preferred_element_type=F32, precision=precision)


def _dot_tn(a, b, precision=None):
    return lax.dot_general(a, b, (((0,), (0,)), ((), ())), preferred_element_type=F32, precision=precision)


def _norm_mm_body(*refs, has_norm, has_res, precision):
    it = iter(refs)
    x_ref = next(it)
    g_ref = next(it) if has_norm else None
    w_ref = next(it)
    res_ref = next(it) if has_res else None
    o_ref = next(it)
    xn_ref = next(it)

    @pl.when(pl.program_id(1) == 0)
    def _():
        x = x_ref[...].astype(F32)
        if has_norm:
            x = x * lax.rsqrt(jnp.mean(x * x, axis=-1, keepdims=True) + NORM_EPS) * g_ref[...]
        xn_ref[...] = x.astype(xn_ref.dtype)

    acc = _dot(xn_ref[...], w_ref[...].astype(xn_ref.dtype), precision)
    if has_res:
        acc = acc + res_ref[...]
    o_ref[...] = acc.astype(o_ref.dtype)


def _norm_mm(x, w, *, norm_g=None, res=None, tm=MM_ROW_TILE, tn=MM_COL_TILE, out_dtype=F32, mxu_dtype=BF16,
             precision=None):
    m, k = x.shape
    n = w.shape[1]
    tm, tn = min(tm, m), min(tn, n)
    assert m % tm == 0
    in_specs = [pl.BlockSpec((tm, k), lambda i, j: (i, 0))]
    args = [x]
    if norm_g is not None:
        in_specs.append(pl.BlockSpec((1, k), lambda i, j: (0, 0)))
        args.append(norm_g.reshape(1, k))
    in_specs.append(pl.BlockSpec((k, tn), lambda i, j: (0, j)))
    args.append(w)
    if res is not None:
        in_specs.append(pl.BlockSpec((tm, tn), lambda i, j: (i, j)))
        args.append(res)
    return pl.pallas_call(
        functools.partial(_norm_mm_body, has_norm=norm_g is not None, has_res=res is not None, precision=precision),
        grid=(m // tm, pl.cdiv(n, tn)),
        in_specs=in_specs,
        out_specs=pl.BlockSpec((tm, tn), lambda i, j: (i, j)),
        out_shape=jax.ShapeDtypeStruct((m, n), out_dtype),
        scratch_shapes=[pltpu.VMEM((tm, k), mxu_dtype)],
        compiler_params=_params("arbitrary", "arbitrary"),
    )(*args)


def _rwkv_chunk_body(r_ref, k_ref, v_ref, wlo_ref, alo_ref, glo_ref,
                     w0_ref, wup_ref, a0_ref, aup_ref, gup_ref, kk_ref, ka_ref,
                     m_ref, n_ref, ry_ref, y0_ref, kmod_ref, g_ref):
    L, N = CHUNK, RWKV_HEAD_DIM
    r = r_ref[...]
    k = k_ref[...]
    v = v_ref[...]
    log_w = -jax.nn.softplus(-(w0_ref[...] + _dot(jnp.tanh(wlo_ref[...]), wup_ref[...], HIGHEST))) - 0.5
    lw = -jnp.exp(log_w)
    a = jax.nn.sigmoid(a0_ref[...] + _dot(alo_ref[...], aup_ref[...], HIGHEST))
    g_ref[...] = _dot(jax.nn.sigmoid(glo_ref[...]), gup_ref[...], HIGHEST)
    k_mod = k * (1.0 + (a - 1.0) * ka_ref[...])
    kmod_ref[...] = k_mod
    kk_raw = k * kk_ref[...]

    row = lax.broadcasted_iota(jnp.int32, (L, L), 0)
    col = lax.broadcasted_iota(jnp.int32, (L, L), 1)
    cum = _dot((col <= row).astype(F32), lw, HIGHEST)
    cum_last = cum[L - 1:L, :]
    g_in = jnp.exp(cum)
    g_prev = jnp.exp(cum - lw)
    g_inv = jnp.exp(-cum)
    g_tail = jnp.exp(cum_last - cum)
    g_last = jnp.exp(cum_last)

    S = RWKV_GROUP
    stack_mask = _head_block_mask(S)
    t_idx = lax.broadcasted_iota(jnp.int32, (L, S), 0)
    s_idx = lax.broadcasted_iota(jnp.int32, (L, S), 1) % N
    incl = s_idx <= t_idx
    strict = s_idx < t_idx
    eye = (s_idx == t_idx).astype(F32)
    blk16 = (t_idx // 16) == (s_idx // 16)
    blk32 = (t_idx // 32) == (s_idx // 32)
    stack = functools.partial(_stack4, mask=stack_mask)
    unstack = functools.partial(_unstack4, mask=stack_mask)

    slabs = [slice(i * S, (i + 1) * S) for i in range(RWKV_WIDTH // S)]
    each = lambda fn, *lists: [fn(*args) for args in zip(*lists)]
    ones_bd = stack_mask.astype(F32)

    kk_g = each(lambda sl: kk_raw[:, sl], slabs)
    ssq = each(lambda x: _dot(x * x, ones_bd, HIGHEST), kk_g)
    kk_g = each(lambda x, s2: x / jnp.maximum(jnp.sqrt(s2), 1e-12), kk_g, ssq)
    b_g = each(lambda x, sl: x * a[:, sl], kk_g, slabs)
    rg = each(lambda sl: r[:, sl] * g_in[:, sl], slabs)
    kkg = each(lambda x, sl: x * g_prev[:, sl], kk_g, slabs)
    v_s = each(lambda sl: stack(v[:, sl]), slabs)
    kd_s = each(lambda sl: stack(k_mod[:, sl] * g_inv[:, sl]), slabs)
    bd_s = each(lambda x, sl: stack(x * g_inv[:, sl]), b_g, slabs)
    kdg = each(lambda sl: _bf(k_mod[:, sl] * g_tail[:, sl]), slabs)
    bdg = each(lambda x, sl: _bf(x * g_tail[:, sl]), b_g, slabs)

    lhs = each(lambda x, y: _bf(jnp.concatenate([x, y], axis=0)), kkg, rg)
    with_k = each(_dot_nt, lhs, kd_s)
    with_b = each(_dot_nt, lhs, bd_s)
    a_k = each(lambda x: jnp.where(strict, x[:L], 0.0), with_k)
    a_b = each(lambda x: jnp.where(strict, x[:L], 0.0), with_b)
    ar_k = each(lambda x: _bf(jnp.where(incl, x[L:], 0.0)), with_k)
    ar_b = each(lambda x: _bf(jnp.where(incl, x[L:], 0.0)), with_b)

    pw = each(lambda x: jnp.where(blk16, -x, 0.0), a_b)
    t_inv = each(lambda x: eye + x, pw)
    pw_s = each(stack, pw)
    for _ in range(3):
        pw = each(lambda x, xs: _dot(_bf(x), xs), pw, pw_s)
        pw_s = each(stack, pw)
        t_inv = each(lambda t, xs: t + _dot(_bf(t), xs), t_inv, pw_s)
    for off_mask in (blk32 & (~blk16), ~blk32):
        off_s = each(lambda x: stack(jnp.where(off_mask, x, 0.0)), a_b)
        left = each(lambda t, o: _bf(_dot(_bf(t), o)), t_inv, off_s)
        t_inv = each(lambda t, lt: t - _dot(lt, stack(t)), t_inv, left)
    t_b = each(_bf, t_inv)

    p = each(lambda t, x: _dot(t, stack(x)), t_b, kkg)
    akv = each(lambda x, vs: _dot(_bf(x), vs), a_k, v_s)
    q = each(lambda t, x: _dot(t, stack(x)), t_b, akv)
    for i, sl in enumerate(slabs):
        ry_ref[:, sl] = rg[i] - _dot(ar_b[i], stack(p[i]))
        y0_ref[:, sl] = _dot(ar_k[i], v_s[i]) - _dot(ar_b[i], stack(q[i]))
        m_ref[:, sl] = eye * g_last[:, sl] - unstack(_dot_tn(bdg[i], _bf(p[i])))
        n_ref[:, sl] = unstack(_dot_tn(kdg[i], _bf(v[:, sl])) - _dot_tn(bdg[i], _bf(q[i])))


def _head_block_mask(s):
    row = lax.broadcasted_iota(jnp.int32, (s, s), 0)
    col = lax.broadcasted_iota(jnp.int32, (s, s), 1)
    return (row // RWKV_HEAD_DIM) == (col // RWKV_HEAD_DIM)


def _stack4(x, mask, dtype=BF16):
    reps = mask.shape[0] // x.shape[0]
    return jnp.where(mask, jnp.concatenate([x] * reps, axis=0), 0.0).astype(dtype)


def _unstack4(x, mask):
    L = RWKV_HEAD_DIM
    x = jnp.where(mask, x, 0.0)
    return sum(x[i * L:(i + 1) * L] for i in range(x.shape[0] // L))


def _rwkv_scan_body(m_ref, n_ref, ry_ref, y0_ref, r_ref, kmod_ref, v_ref, g_ref, rk_ref, lng_ref, lnb_ref,
                    o_ref, h_ref):
    S = RWKV_GROUP
    mask = _head_block_mask(S)
    head_mean = mask.astype(F32) * (1.0 / RWKV_HEAD_DIM)

    @pl.when(pl.program_id(0) == 0)
    def _():
        h_ref[...] = jnp.zeros_like(h_ref)

    slabs = [slice(i * S, (i + 1) * S) for i in range(RWKV_WIDTH // S)]
    each = lambda fn, *lists: [fn(*args) for args in zip(*lists)]
    state = [_bf(h_ref[i]) for i in range(len(slabs))]
    y = each(lambda sl, h: _dot(_bf(ry_ref[:, sl]), h) + y0_ref[:, sl], slabs, state)
    for i, sl in enumerate(slabs):
        h_ref[i] = _dot(_stack4(m_ref[:, sl], mask), state[i]) + _stack4(n_ref[:, sl], mask, F32)
    mean = each(lambda x: _dot(x, head_mean, HIGHEST), y)
    var = each(lambda x, mu: _dot(jnp.square(x - mu), head_mean, HIGHEST), y, mean)
    rk = each(lambda sl: _dot(r_ref[:, sl] * kmod_ref[:, sl] * rk_ref[:, sl], mask.astype(F32), HIGHEST), slabs)
    for i, sl in enumerate(slabs):
        yn = (y[i] - mean[i]) * lax.rsqrt(var[i] + GN_EPS) * lng_ref[:, sl] + lnb_ref[:, sl]
        o_ref[:, sl] = ((yn + rk[i] * v_ref[:, sl]) * g_ref[:, sl]).astype(o_ref.dtype)


def _rwkv7(pa, lora, w0, w_up, a0, a_up, g_up, k_k, k_a, r_k, ln_g, ln_b, out_dtype=F32):
    t = pa.shape[0]
    nc = t // CHUNK
    W, L = RWKV_WIDTH, CHUNK
    c = 0
    row_w = lambda i: (i, 0)
    vec = lambda x: x.reshape(1, W)
    pad_rows = lambda x: jnp.pad(x, ((0, LORA_PAD - x.shape[0]), (0, 0)))
    full = lambda shape: pl.BlockSpec(shape, lambda i: (0,) * len(shape))
    m, n, ry, y0, kmod, g = pl.pallas_call(
        _rwkv_chunk_body,
        grid=(nc,),
        in_specs=[
            pl.BlockSpec((L, W), lambda i: (i, c)),
            pl.BlockSpec((L, W), lambda i: (i, c + 1)),
            pl.BlockSpec((L, W), lambda i: (i, c + 2)),
            pl.BlockSpec((L, LORA_PAD), lambda i: (i, 0)),
            pl.BlockSpec((L, LORA_PAD), lambda i: (i, 1)),
            pl.BlockSpec((L, GATE_LORA), lambda i: (i, 1)),
            full((1, W)), full((LORA_PAD, W)), full((1, W)), full((LORA_PAD, W)), full((GATE_LORA, W)),
            full((1, W)), full((1, W)),
        ],
        out_specs=[
            pl.BlockSpec((L, W), row_w), pl.BlockSpec((L, W), row_w),
            pl.BlockSpec((L, W), row_w), pl.BlockSpec((L, W), row_w),
            pl.BlockSpec((L, W), row_w), pl.BlockSpec((L, W), row_w),
        ],
        out_shape=[
            jax.ShapeDtypeStruct((t, W), F32), jax.ShapeDtypeStruct((t, W), F32),
            jax.ShapeDtypeStruct((t, W), F32), jax.ShapeDtypeStruct((t, W), F32),
            jax.ShapeDtypeStruct((t, W), F32), jax.ShapeDtypeStruct((t, W), F32),
        ],
        compiler_params=_params("arbitrary"),
    )(pa, pa, pa, lora, lora, lora, vec(w0), pad_rows(w_up), vec(a0), pad_rows(a_up), g_up, vec(k_k), vec(k_a))

    return pl.pallas_call(
        _rwkv_scan_body,
        grid=(nc,),
        in_specs=[
            pl.BlockSpec((L, W), row_w), pl.BlockSpec((L, W), row_w),
            pl.BlockSpec((L, W), row_w), pl.BlockSpec((L, W), row_w),
            pl.BlockSpec((L, W), lambda i: (i, c)),
            pl.BlockSpec((L, W), row_w),
            pl.BlockSpec((L, W), lambda i: (i, c + 2)),
            pl.BlockSpec((L, W), row_w),
            full((1, W)), full((1, W)), full((1, W)),
        ],
        out_specs=pl.BlockSpec((L, W), row_w),
        out_shape=jax.ShapeDtypeStruct((t, W), out_dtype),
        scratch_shapes=[pltpu.VMEM((W // RWKV_GROUP, RWKV_GROUP, RWKV_GROUP), F32)],
        compiler_params=_params("arbitrary"),
    )(m, n, ry, y0, pa, kmod, pa, g, vec(r_k), vec(ln_g), vec(ln_b))


def _mlstm_body(q_ref, k_ref, v_ref, o_ref, gates_ref, bif_ref, ng_ref, out_ref, c_ref, n_ref, m_ref):
    L, D = CHUNK, MLSTM_HEAD_DIM

    @pl.when(pl.program_id(0) == 0)
    def _():
        c_ref[...] = jnp.zeros_like(c_ref)
        n_ref[...] = jnp.zeros_like(n_ref)
        m_ref[...] = jnp.zeros_like(m_ref)

    row = lax.broadcasted_iota(jnp.int32, (L, L), 0)
    col = lax.broadcasted_iota(jnp.int32, (L, L), 1)
    incl = col <= row
    eye = (col == row).astype(F32)
    gates = gates_ref[...] + bif_ref[...]
    b_all = _dot(incl.astype(F32), jax.nn.log_sigmoid(gates), HIGHEST)
    to_row = lambda c: jnp.sum(c * eye, axis=0, keepdims=True)

    for h in range(MLSTM_HEADS):
        sl = slice(h * D, (h + 1) * D)
        q = q_ref[:, sl] * (D ** -0.5)
        k = k_ref[:, sl]
        v = v_ref[:, sl]
        i_col = gates[:, h:h + 1]
        b_col = b_all[:, MLSTM_HEADS + h:MLSTM_HEADS + h + 1]
        m_prev = m_ref[h]
        c_mat = c_ref[h]
        n_vec = n_ref[h]
        d_mat = jnp.where(incl, b_col - to_row(b_col) + to_row(i_col), -jnp.inf)
        inter = b_col + m_prev
        m_row = jnp.maximum(inter, jnp.max(d_mat, axis=-1, keepdims=True))
        w_inter = jnp.exp(inter - m_row)
        q_b, k_b = _bf(q), _bf(k)
        s = _dot_nt(q_b, k_b) * jnp.exp(d_mat - m_row)
        num = _dot(_bf(s), _bf(v)) + w_inter * _dot_nt(q_b, _bf(c_mat))
        den = jnp.sum(s, axis=-1, keepdims=True) + w_inter * jnp.sum(q * n_vec, axis=-1, keepdims=True)
        hh = num / jnp.maximum(jnp.abs(den), jnp.exp(-m_row))
        b_last = b_col[L - 1:L, :]
        d_state = b_last - b_col + i_col
        m_new = jnp.maximum(b_last + m_prev, jnp.max(d_state, axis=0, keepdims=True))
        w_state = jnp.exp(d_state - m_new)
        w_carry = jnp.exp(b_last + m_prev - m_new)
        c_ref[h] = w_carry * c_mat + _dot_tn(_bf(w_state * v), k_b)
        n_ref[h] = w_carry * n_vec + jnp.sum(w_state * k, axis=0, keepdims=True)
        m_ref[h] = m_new
        hn = hh * lax.rsqrt(jnp.mean(hh * hh, axis=-1, keepdims=True) + NORM_EPS) * ng_ref[:, sl]
        out_ref[:, sl] = (hn * jax.nn.sigmoid(o_ref[:, sl])).astype(out_ref.dtype)


def _mlstm(qk, proj, v_blk, o_blk, gates_blk, b_if, norm_g, out_dtype=F32):
    t = qk.shape[0]
    L, W = CHUNK, MLSTM_WIDTH
    bif = jnp.pad(b_if, (0, LORA_PAD - b_if.shape[0])).reshape(1, LORA_PAD)
    return pl.pallas_call(
        _mlstm_body,
        grid=(t // L,),
        in_specs=[
            pl.BlockSpec((L, W), lambda i: (i, 0)),
            pl.BlockSpec((L, W), lambda i: (i, 1)),
            pl.BlockSpec((L, W), lambda i: (i, v_blk)),
            pl.BlockSpec((L, W), lambda i: (i, o_blk)),
            pl.BlockSpec((L, LORA_PAD), lambda i: (i, gates_blk)),
            pl.BlockSpec((1, LORA_PAD), lambda i: (0, 0)),
            pl.BlockSpec((1, W), lambda i: (0, 0)),
        ],
        out_specs=pl.BlockSpec((L, W), lambda i: (i, 0)),
        out_shape=jax.ShapeDtypeStruct((t, W), out_dtype),
        scratch_shapes=[pltpu.VMEM((MLSTM_HEADS, MLSTM_HEAD_DIM, MLSTM_HEAD_DIM), F32),
                        pltpu.VMEM((MLSTM_HEADS, 1, MLSTM_HEAD_DIM), F32),
                        pltpu.VMEM((MLSTM_HEADS, 1, 1), F32)],
        compiler_params=_params("arbitrary"),
    )(qk, qk, proj, proj, proj, bif, norm_g.reshape(1, W))


def _fox_qkv_prep_body(x_ref, g_ref, o_ref):
    D = FOX_HEAD_DIM

    @pl.when(pl.program_id(1) < 2)
    def _():
        for h in range(FOX_HEADS):
            sl = slice(h * D, (h + 1) * D)
            x = x_ref[:, sl]
            y = x * lax.rsqrt(jnp.mean(x * x, axis=-1, keepdims=True) + NORM_EPS) * g_ref[...]
            o_ref[:, sl] = y.astype(o_ref.dtype)

    @pl.when(pl.program_id(1) == 2)
    def _():
        o_ref[...] = x_ref[...].astype(o_ref.dtype)


def _fox_qkv_prep(proj, qn_g, kn_g, tm=512):
    t = proj.shape[0]
    W = FOX_HEADS * FOX_HEAD_DIM
    gains = jnp.stack([qn_g * (FOX_HEAD_DIM ** -0.5 * LOG2E), kn_g, jnp.ones_like(kn_g)]).reshape(3, 1, FOX_HEAD_DIM)
    return pl.pallas_call(
        _fox_qkv_prep_body,
        grid=(t // tm, 3),
        in_specs=[pl.BlockSpec((tm, W), lambda i, j: (i, j)),
                  pl.BlockSpec((None, 1, FOX_HEAD_DIM), lambda i, j: (j, 0, 0))],
        out_specs=pl.BlockSpec((tm, W), lambda i, j: (i, j)),
        out_shape=jax.ShapeDtypeStruct((t, 3 * W), BF16),
        compiler_params=_params("arbitrary", "arbitrary"),
    )(proj, gains)


def _log_forget_cumsum_body(f_ref, b_ref, o_ref, carry_ref):
    tm = f_ref.shape[0]

    @pl.when(pl.program_id(0) == 0)
    def _():
        carry_ref[...] = jnp.zeros_like(carry_ref)

    row = lax.broadcasted_iota(jnp.int32, (tm, tm), 0)
    col = lax.broadcasted_iota(jnp.int32, (tm, tm), 1)
    log_f = jax.nn.log_sigmoid(f_ref[...] + b_ref[...])
    cum = _dot((col <= row).astype(F32), log_f, HIGHEST) + carry_ref[...]
    o_ref[...] = cum * LOG2E
    carry_ref[...] = cum[tm - 1:tm, :]


def _log_forget_cumsum(f_pre, b_f, tm=512):
    t, nh = f_pre.shape
    return pl.pallas_call(
        _log_forget_cumsum_body,
        grid=(t // tm,),
        in_specs=[pl.BlockSpec((tm, nh), lambda i: (i, 0)), pl.BlockSpec((1, nh), lambda i: (0, 0))],
        out_specs=pl.BlockSpec((tm, nh), lambda i: (i, 0)),
        out_shape=jax.ShapeDtypeStruct((t, nh), F32),
        scratch_shapes=[pltpu.VMEM((1, nh), F32)],
        compiler_params=_params("arbitrary"),
    )(f_pre, b_f.reshape(1, nh))


def _fox_body(q_ref, k_ref, v_ref, g_ref, cq_ref, ck_ref, o_ref, *, tq, tk, ts):
    qi = pl.program_id(1)
    n_sub = tq // ts

    def block(kb, carry, diagonal):
        start = pl.multiple_of(kb * tk, tk)
        subs = range(n_sub)
        width = [(i + 1) * ts if diagonal else tk for i in subs]
        rows = [pl.ds(i * ts, ts) for i in subs]
        s = [_dot_nt(q_ref[rows[i], :], k_ref[pl.ds(start, width[i]), :]) for i in subs]
        s = [s[i] + cq_ref[rows[i], :] - ck_ref[kb][:, :width[i]] for i in subs]
        if diagonal:
            s = [jnp.where(lax.broadcasted_iota(jnp.int32, (ts, width[i]), 1)
                           <= i * ts + lax.broadcasted_iota(jnp.int32, (ts, width[i]), 0), s[i], -jnp.inf)
                 for i in subs]
        m_new = [jnp.maximum(carry[i][0], jnp.max(s[i], axis=-1, keepdims=True)) for i in subs]
        p = [jnp.exp2(s[i] - m_new[i]) for i in subs]
        alpha = [jnp.exp2(carry[i][0] - m_new[i]) for i in subs]
        l_new = [alpha[i] * carry[i][1] + jnp.sum(p[i], axis=-1, keepdims=True) for i in subs]
        pv = [_dot(_bf(p[i]), v_ref[pl.ds(start, width[i]), :]) for i in subs]
        return tuple((m_new[i], l_new[i], alpha[i] * carry[i][2] + pv[i]) for i in subs)

    init = tuple((jnp.full((ts, 1), -jnp.inf, F32), jnp.zeros((ts, 1), F32), jnp.zeros((ts, FOX_HEAD_DIM), F32))
                 for _ in range(n_sub))
    carry = lax.fori_loop(0, qi, lambda kb, c: block(kb, c, False), init)
    final = block(qi, carry, True)
    for i in range(n_sub):
        _, l_fin, acc = final[i]
        rows = pl.ds(i * ts, ts)
        o_ref[rows, :] = (acc / l_fin * jax.nn.sigmoid(g_ref[rows, :])).astype(o_ref.dtype)


def _fox_attention(qkv, proj, g_blk0, cum, tq=512):
    t = qkv.shape[0]
    D, H = FOX_HEAD_DIM, FOX_HEADS
    tk = tq
    cum_t = cum.T
    cq = cum_t.reshape(H, t, 1)
    ck = cum_t.reshape(H, t // tk, 1, tk)
    return pl.pallas_call(
        functools.partial(_fox_body, tq=tq, tk=tk, ts=FOX_SUB_TILE),
        grid=(H, t // tq),
        in_specs=[
            pl.BlockSpec((tq, D), lambda h, qi: (qi, h)),
            pl.BlockSpec((t, D), lambda h, qi: (0, H + h)),
            pl.BlockSpec((t, D), lambda h, qi: (0, 2 * H + h)),
            pl.BlockSpec((tq, D), lambda h, qi: (qi, g_blk0 + h)),
            pl.BlockSpec((None, tq, 1), lambda h, qi: (h, qi, 0)),
            pl.BlockSpec((None, t // tk, 1, tk), lambda h, qi: (h, 0, 0, 0)),
        ],
        out_specs=pl.BlockSpec((tq, D), lambda h, qi: (qi, h)),
        out_shape=jax.ShapeDtypeStruct((t, H * D), BF16),
        compiler_params=_params("arbitrary", "arbitrary"),
    )(qkv, qkv, qkv, proj, cq, ck)


def _xattn_body(q_ref, k_ref, v_ref, o_ref):
    s = _dot_nt(q_ref[...], k_ref[...]) * (XATTN_HEAD_DIM ** -0.5)
    p = jnp.exp(s - jnp.max(s, axis=-1, keepdims=True))
    o = _dot(p.astype(BF16), v_ref[...]) / jnp.sum(p, axis=-1, keepdims=True)
    o_ref[...] = o.astype(o_ref.dtype)


def _cross_attention(q, kv, tq=512):
    t = q.shape[0]
    n_mem = kv.shape[0]
    D, H = XATTN_HEAD_DIM, XATTN_HEADS
    return pl.pallas_call(
        _xattn_body,
        grid=(t // tq, H),
        in_specs=[pl.BlockSpec((tq, D), lambda i, h: (i, h)),
                  pl.BlockSpec((n_mem, D), lambda i, h: (0, h)),
                  pl.BlockSpec((n_mem, D), lambda i, h: (0, H + h))],
        out_specs=pl.BlockSpec((tq, D), lambda i, h: (i, h)),
        out_shape=jax.ShapeDtypeStruct((t, H * D), BF16),
        compiler_params=_params("arbitrary", "arbitrary"),
    )(q, kv, kv)


def _swiglu_up_body(x_ref, g_ref, wg_ref, wu_ref, o_ref, xn_ref):
    @pl.when(pl.program_id(1) == 0)
    def _():
        x = x_ref[...]
        xn_ref[...] = _bf(x * lax.rsqrt(jnp.mean(x * x, axis=-1, keepdims=True) + NORM_EPS) * g_ref[...])

    xn = xn_ref[...]
    a = _dot(xn, _bf(wg_ref[...]))
    b = _dot(xn, _bf(wu_ref[...]))
    o_ref[...] = (a * jax.nn.sigmoid(a) * b).astype(o_ref.dtype)


def _swiglu_up(x, norm_g, w_gate, w_up, tm=MM_ROW_TILE, tf=MM_COL_TILE):
    t, d = x.shape
    f = w_gate.shape[1]
    tm = min(tm, t)
    return pl.pallas_call(
        _swiglu_up_body,
        grid=(t // tm, f // tf),
        in_specs=[pl.BlockSpec((tm, d), lambda i, j: (i, 0)),
                  pl.BlockSpec((1, d), lambda i, j: (0, 0)),
                  pl.BlockSpec((d, tf), lambda i, j: (0, j)),
                  pl.BlockSpec((d, tf), lambda i, j: (0, j))],
        out_specs=pl.BlockSpec((tm, tf), lambda i, j: (i, j)),
        out_shape=jax.ShapeDtypeStruct((t, f), BF16),
        scratch_shapes=[pltpu.VMEM((tm, d), BF16)],
        compiler_params=_params("arbitrary", "arbitrary"),
    )(x, norm_g.reshape(1, d), w_gate, w_up)


def _mm_acc_body(x_ref, w_ref, res_ref, o_ref):
    @pl.when(pl.program_id(2) == 0)
    def _():
        o_ref[...] = res_ref[...]

    o_ref[...] += _dot(x_ref[...], _bf(w_ref[...]))


def _mm_acc(x, w, res, tm=MM_ROW_TILE, tn=MM_ROW_TILE, tk=MM_COL_TILE):
    t, k = x.shape
    n = w.shape[1]
    tm = min(tm, t)
    return pl.pallas_call(
        _mm_acc_body,
        grid=(t // tm, n // tn, k // tk),
        in_specs=[pl.BlockSpec((tm, tk), lambda i, j, kk: (i, kk)),
                  pl.BlockSpec((tk, tn), lambda i, j, kk: (kk, j)),
                  pl.BlockSpec((tm, tn), lambda i, j, kk: (i, j))],
        out_specs=pl.BlockSpec((tm, tn), lambda i, j, kk: (i, j)),
        out_shape=jax.ShapeDtypeStruct((t, n), F32),
        compiler_params=_params("arbitrary", "arbitrary", "arbitrary"),
    )(x, w, res)


def _rmsnorm_body(x_ref, g_ref, o_ref):
    x = x_ref[...]
    o_ref[...] = x * lax.rsqrt(jnp.mean(x * x, axis=-1, keepdims=True) + NORM_EPS) * g_ref[...]


def _rmsnorm(x, g, tm=512):
    t, d = x.shape
    return pl.pallas_call(
        _rmsnorm_body,
        grid=(t // tm,),
        in_specs=[pl.BlockSpec((tm, d), lambda i: (i, 0)), pl.BlockSpec((1, d), lambda i: (0, 0))],
        out_specs=pl.BlockSpec((tm, d), lambda i: (i, 0)),
        out_shape=jax.ShapeDtypeStruct((t, d), F32),
        compiler_params=_params("arbitrary"),
    )(x, g.reshape(1, d))


def _pad_cols(w, n=LORA_PAD):
    return jnp.pad(w, [(0, 0)] * (w.ndim - 1) + [(0, n - w.shape[-1])])


def _pack_ab_columns(w):
    a_cols = 3 * RWKV_WIDTH + DECAY_LORA + AAA_LORA + GATE_LORA
    a, b = w[..., :a_cols], w[..., a_cols:]
    o1 = 3 * RWKV_WIDTH
    o2 = o1 + DECAY_LORA
    o3 = o2 + AAA_LORA
    return jnp.concatenate([
        a[..., :o1], b[..., :4 * MLSTM_WIDTH],
        _pad_cols(a[..., o1:o2]), _pad_cols(a[..., o2:o3]), a[..., o3:],
        _pad_cols(b[..., 4 * MLSTM_WIDTH:])], axis=-1)


def _rwkv_mlstm_mixer(x, norm_g, w_in, mu, w0, w_up, a0, a_up, g_up, k_k, k_a, r_k, ln_g, ln_b,
                      conv_qk, b_if, mh_g, w_out):
    t = x.shape[0]
    proj = _norm_mm(x, _pack_ab_columns(w_in), norm_g=norm_g)
    mu_p = _pack_ab_columns(jnp.concatenate([mu, jnp.zeros((4 * MLSTM_WIDTH + 2 * MLSTM_HEADS,), F32)]))
    o1 = 3 * RWKV_WIDTH
    o_lora = o1 + 4 * MLSTM_WIDTH

    def token_shift(p, m):
        prev = jnp.concatenate([jnp.zeros((1, p.shape[1]), p.dtype), p[:-1]], axis=0)
        return p + (prev - p) * m

    pa = token_shift(proj[:, :o1], mu_p[:o1])
    lora = token_shift(proj[:, o_lora:o_lora + 512], mu_p[o_lora:o_lora + 512])
    y_a = _rwkv7(pa, lora, w0, w_up, a0, a_up, g_up, k_k, k_a, r_k, ln_g, ln_b, out_dtype=BF16)

    qk = proj[:, o1:o1 + 2 * MLSTM_WIDTH]
    qk_pad = jnp.concatenate([jnp.zeros((MLSTM_CONV - 1, qk.shape[1]), F32), qk], axis=0)
    conv = sum(conv_qk[j] * qk_pad[j:j + t] for j in range(MLSTM_CONV))
    qk_c = conv * jax.nn.sigmoid(conv)
    y_b = _mlstm(qk_c, proj, 5, 6, (o_lora + 512) // LORA_PAD, b_if, mh_g, out_dtype=BF16)

    x = _norm_mm(y_a, w_out[:RWKV_WIDTH], res=x)
    return _norm_mm(y_b, w_out[RWKV_WIDTH:], res=x)


def _fox_layer(x, norm_g, w_in, b_f, qn_g, kn_g, w_out):
    W = FOX_HEADS * FOX_HEAD_DIM
    proj = _norm_mm(x, w_in, norm_g=norm_g)
    qkv = _fox_qkv_prep(proj, qn_g, kn_g)
    cum = _log_forget_cumsum(proj[:, 4 * W:], b_f)
    o = _fox_attention(qkv, proj, 3 * FOX_HEADS, cum)
    return _norm_mm(o, w_out, res=x)


def _xattn_layer(x, norm_g, kv, wq, wo):
    q = _norm_mm(x, wq, norm_g=norm_g, out_dtype=BF16)
    return _norm_mm(_cross_attention(q, kv), wo, res=x)


def _moe_experts_body(tile_e_ref, n_valid_ref, x_ref, g_ref, wg_ref, wu_ref, wd_ref, gate_ref, o_ref, xn_ref):
    i = pl.program_id(0)
    j = pl.program_id(1)

    @pl.when(j == 0)
    def _():
        o_ref[...] = jnp.zeros_like(o_ref)
        x = x_ref[...]
        xn_ref[...] = _bf(x * lax.rsqrt(jnp.mean(x * x, axis=-1, keepdims=True) + NORM_EPS) * g_ref[...])

    @pl.when(i < n_valid_ref[0])
    def _():
        x = xn_ref[...]
        a = _dot(x, _bf(wg_ref[...]))
        b = _dot(x, _bf(wu_ref[...]))
        h = a * jax.nn.sigmoid(a) * b
        o_ref[...] += _dot(_bf(h), _bf(wd_ref[...]))

    @pl.when(j == pl.num_programs(1) - 1)
    def _():
        o_ref[...] = o_ref[...] * gate_ref[...]


def _moe_experts(xg, norm_g, gate_at, tile_e, n_valid, w_gate, w_up, w_down, tm, tf=256):
    p, d = xg.shape
    f = w_gate.shape[2]
    nf = f // tf

    def expert(i, te, nv):
        return te[jnp.minimum(i, nv[0] - 1)]

    def fblock(i, j, nv):
        return jnp.where(i < nv[0], j, nf - 1)

    grid_spec = pltpu.PrefetchScalarGridSpec(
        num_scalar_prefetch=2,
        grid=(p // tm, nf),
        in_specs=[
            pl.BlockSpec((tm, d), lambda i, j, te, nv: (i, 0), pipeline_mode=pl.Buffered(1)),
            pl.BlockSpec((1, d), lambda i, j, te, nv: (0, 0)),
            pl.BlockSpec((None, d, tf), lambda i, j, te, nv: (expert(i, te, nv), 0, fblock(i, j, nv))),
            pl.BlockSpec((None, d, tf), lambda i, j, te, nv: (expert(i, te, nv), 0, fblock(i, j, nv))),
            pl.BlockSpec((None, tf, d), lambda i, j, te, nv: (expert(i, te, nv), fblock(i, j, nv), 0)),
            pl.BlockSpec((tm, 1), lambda i, j, te, nv: (i, 0)),
        ],
        out_specs=pl.BlockSpec((tm, d), lambda i, j, te, nv: (i, 0)),
        scratch_shapes=[pltpu.VMEM((tm, d), BF16)],
    )
    return pl.pallas_call(
        _moe_experts_body,
        grid_spec=grid_spec,
        out_shape=jax.ShapeDtypeStruct((p, d), F32),
        compiler_params=_params("arbitrary", "arbitrary"),
    )(tile_e, n_valid, xg, norm_g.reshape(1, d), w_gate, w_up, w_down, gate_at)


MOE_TILE = 1024


def _moe_layer(x, norm_g, w_router, w_gate, w_up, w_down):
    t = x.shape[0]
    tm = MOE_TILE
    logits = _norm_mm(x, _pad_cols(w_router), norm_g=norm_g, tm=512, mxu_dtype=F32, precision=HIGHEST)
    top_val, top_idx = lax.top_k(logits[:, :N_EXPERTS], TOP_K)
    gates = jax.nn.softmax(top_val, axis=-1)
    flat_e = top_idx.reshape(-1)
    onehot = (flat_e[:, None] == jnp.arange(N_EXPERTS)[None, :]).astype(jnp.int32)
    ranks = jnp.cumsum(onehot, axis=0) - onehot
    counts = jnp.sum(onehot, axis=0)
    padded = (counts + tm - 1) // tm * tm
    pad_end = jnp.cumsum(padded)
    dest = (pad_end - padded)[flat_e] + jnp.sum(ranks * onehot, axis=1)
    n_rows = TOP_K * t + N_EXPERTS * tm
    tok_at = jnp.zeros((n_rows,), jnp.int32).at[dest].set(jnp.arange(TOP_K * t, dtype=jnp.int32) // TOP_K)
    gate_at = jnp.zeros((n_rows,), F32).at[dest].set(gates.reshape(-1))
    tile_start = jnp.arange(n_rows // tm, dtype=jnp.int32) * tm
    tile_e = jnp.minimum(jnp.sum((tile_start[:, None] >= pad_end[None, :]).astype(jnp.int32), axis=1), N_EXPERTS - 1)
    n_valid = (pad_end[-1:] // tm).astype(jnp.int32)
    xg = jnp.take(x, tok_at, axis=0)
    y = _moe_experts(xg, norm_g, gate_at.reshape(n_rows, 1), tile_e, n_valid, w_gate, w_up, w_down, tm)
    dest2 = dest.reshape(t, TOP_K)
    return x + jnp.take(y, dest2[:, 0], axis=0) + jnp.take(y, dest2[:, 1], axis=0)


def kernel(x, mem, mem_norm_g, final_norm_g, norm_mix_g, norm_xattn_g, norm_ffn_g,
           xa_wq, xa_wkv, xa_wo,
           ab_w_in, ab_mu, rw_w0, rw_w_up, rw_a0, rw_a_up, rw_g_up, rw_k_k, rw_k_a, rw_r_k,
           rw_ln_g, rw_ln_b, ml_conv, ml_b_if, ml_norm_g, ab_w_out,
           ffn_w_gate, ffn_w_up, ffn_w_down,
           fox_w_in, fox_b_f, fox_qn_g, fox_kn_g, fox_w_out,
           moe_router, moe_w_gate, moe_w_up, moe_w_down):
    bsz, seq, d = x.shape
    assert bsz == 1 and d == D_MODEL and seq % 512 == 0
    xs = x[0]
    mem_s = mem[0]
    depth = norm_mix_g.shape[0]
    for layer in range(depth):
        j = layer // 2
        if layer % 2 == 0:
            xs = _rwkv_mlstm_mixer(xs, norm_mix_g[layer], ab_w_in[j], ab_mu[j], rw_w0[j], rw_w_up[j], rw_a0[j],
                                   rw_a_up[j], rw_g_up[j], rw_k_k[j], rw_k_a[j], rw_r_k[j], rw_ln_g[j],
                                   rw_ln_b[j], ml_conv[j], ml_b_if[j], ml_norm_g[j], ab_w_out[j])
        else:
            xs = _fox_layer(xs, norm_mix_g[layer], fox_w_in[j], fox_b_f[j], fox_qn_g[j], fox_kn_g[j],
                            fox_w_out[j])
        kv = _norm_mm(mem_s, xa_wkv[layer], norm_g=mem_norm_g, tm=mem_s.shape[0], out_dtype=BF16)
        xs = _xattn_layer(xs, norm_xattn_g[layer], kv, xa_wq[layer], xa_wo[layer])
        if layer % 2 == 0:
            hidden = _swiglu_up(xs, norm_ffn_g[layer], ffn_w_gate[j], ffn_w_up[j])
            xs = _mm_acc(hidden, ffn_w_down[j], xs)
        else:
            xs = _moe_layer(xs, norm_ffn_g[layer], moe_router[j], moe_w_gate[j], moe_w_up[j], moe_w_down[j])
    return _rmsnorm(xs, final_norm_g)[None]
```

```python
import functools

import jax
import jax.numpy as jnp
from jax import lax
from jax.experimental import pallas as pl
from jax.experimental.pallas import tpu as pltpu

F32 = jnp.float32
BF16 = jnp.bfloat16
HIGHEST = lax.Precision.HIGHEST

D_MODEL = 2048
CHUNK = 64
NORM_EPS = 1e-6
GN_EPS = 64e-5
RWKV_WIDTH = 1024
RWKV_HEAD_DIM = 64
RWKV_HEADS = 16
RWKV_GROUP = 256
DECAY_LORA = 96
AAA_LORA = 96
GATE_LORA = 256
PREV_ROWS = 8
LORA_PAD = 128
MLSTM_WIDTH = 1024
MLSTM_HEAD_DIM = 256
MLSTM_HEADS = 4
MLSTM_CONV = 4
FOX_HEAD_DIM = 128
FOX_HEADS = 16
XATTN_HEADS = 4
XATTN_HEAD_DIM = 512
D_FF = 5632
N_EXPERTS = 8
TOP_K = 2

VMEM_LIMIT_BYTES = 56 * 1024 * 1024
LOG2E = 1.4426950408889634
FOX_SUB_TILE = 128
MM_ROW_TILE = 1024
MM_COL_TILE = 512


def _params(*semantics):
    return pltpu.CompilerParams(dimension_semantics=semantics, vmem_limit_bytes=VMEM_LIMIT_BYTES)


def _bf(a):
    return a.astype(BF16)


def _dot(a, b, precision=None):
    return jnp.dot(a, b, preferred_element_type=F32, precision=precision)


def _dot_nt(a, b, precision=None):
    return lax.dot_general(a, b, (((1,), (1,)), ((), ())), preferred_element_type=F32, precision=precision)


def _dot_tn(a, b, precision=None):
    return lax.dot_general(a, b, (((0,), (0,)), ((), ())), preferred_element_type=F32, precision=precision)


def _norm_mm_body(*refs, has_norm, has_res, precision):
    it = iter(refs)
    x_ref = next(it)
    g_ref = next(it) if has_norm else None
    w_ref = next(it)
    res_ref = next(it) if has_res else None
    o_ref = next(it)
    xn_ref = next(it)

    @pl.when(pl.program_id(1) == 0)
    def _():
        x = x_ref[...].astype(F32)
        if has_norm:
            x = x * lax.rsqrt(jnp.mean(x * x, axis=-1, keepdims=True) + NORM_EPS) * g_ref[...]
        xn_ref[...] = x.astype(xn_ref.dtype)

    acc = _dot(xn_ref[...], w_ref[...].astype(xn_ref.dtype), precision)
    if has_res:
        acc = acc + res_ref[...]
    o_ref[...] = acc.astype(o_ref.dtype)


def _norm_mm(x, w, *, norm_g=None, res=None, tm=MM_ROW_TILE, tn=MM_COL_TILE, out_dtype=F32, mxu_dtype=BF16,
             precision=None):
    m, k = x.shape
    n = w.shape[1]
    tm, tn = min(tm, m), min(tn, n)
    assert m % tm == 0
    in_specs = [pl.BlockSpec((tm, k), lambda i, j: (i, 0))]
    args = [x]
    if norm_g is not None:
        in_specs.append(pl.BlockSpec((1, k), lambda i, j: (0, 0)))
        args.append(norm_g.reshape(1, k))
    in_specs.append(pl.BlockSpec((k, tn), lambda i, j: (0, j)))
    args.append(w)
    if res is not None:
        in_specs.append(pl.BlockSpec((tm, tn), lambda i, j: (i, j)))
        args.append(res)
    return pl.pallas_call(
        functools.partial(_norm_mm_body, has_norm=norm_g is not None, has_res=res is not None, precision=precision),
        grid=(m // tm, pl.cdiv(n, tn)),
        in_specs=in_specs,
        out_specs=pl.BlockSpec((tm, tn), lambda i, j: (i, j)),
        out_shape=jax.ShapeDtypeStruct((m, n), out_dtype),
        scratch_shapes=[pltpu.VMEM((tm, k), mxu_dtype)],
        compiler_params=_params("arbitrary", "arbitrary"),
    )(*args)


def _rwkv_chunk_body(r_ref, k_ref, v_ref, wlo_ref, alo_ref, glo_ref,
                     rp_ref, kp_ref, vp_ref, wlop_ref, alop_ref, glop_ref,
                     mur_ref, muk_ref, muv_ref, muw_ref, mua_ref, mug_ref,
                     w0_ref, wup_ref, a0_ref, aup_ref, gup_ref, kk_ref, ka_ref, rk_ref,
                     m_ref, n_ref, ry_ref, y0_ref, rkk_ref, vs_ref, g_ref):
    L, N = CHUNK, RWKV_HEAD_DIM
    first_chunk = pl.program_id(0) == 0

    def token_shift(x_ref, prev_ref, mu_ref):
        x = x_ref[...]
        before = jnp.where(first_chunk, 0.0, prev_ref[PREV_ROWS - 1:PREV_ROWS, :])
        first_row = lax.broadcasted_iota(jnp.int32, x.shape, 0) == 0
        prev = jnp.where(first_row, before, pltpu.roll(x, 1, axis=0))
        return x + (prev - x) * mu_ref[...]

    r = token_shift(r_ref, rp_ref, mur_ref)
    k = token_shift(k_ref, kp_ref, muk_ref)
    v = token_shift(v_ref, vp_ref, muv_ref)
    w_lo = token_shift(wlo_ref, wlop_ref, muw_ref)
    a_lo = token_shift(alo_ref, alop_ref, mua_ref)
    g_lo = token_shift(glo_ref, glop_ref, mug_ref)
    log_w = -jax.nn.softplus(-(w0_ref[...] + _dot(jnp.tanh(w_lo), wup_ref[...], HIGHEST))) - 0.5
    lw = -jnp.exp(log_w)
    a = jax.nn.sigmoid(a0_ref[...] + _dot(a_lo, aup_ref[...], HIGHEST))
    g_ref[...] = _dot(jax.nn.sigmoid(g_lo), gup_ref[...], HIGHEST)
    k_mod = k * (1.0 + (a - 1.0) * ka_ref[...])
    rkk_ref[...] = r * k_mod * rk_ref[...]
    vs_ref[...] = v
    kk_raw = k * kk_ref[...]

    row = lax.broadcasted_iota(jnp.int32, (L, L), 0)
    col = lax.broadcasted_iota(jnp.int32, (L, L), 1)
    cum = _dot((col <= row).astype(F32), lw, HIGHEST)
    cum_last = cum[L - 1:L, :]
    g_in = jnp.exp(cum)
    g_prev = jnp.exp(cum - lw)
    g_inv = jnp.exp(-cum)
    g_tail = jnp.exp(cum_last - cum)
    g_last = jnp.exp(cum_last)

    S = RWKV_GROUP
    stack_mask = _head_block_mask(S)
    t_idx = lax.broadcasted_iota(jnp.int32, (L, S), 0)
    s_idx = lax.broadcasted_iota(jnp.int32, (L, S), 1) % N
    incl = s_idx <= t_idx
    strict = s_idx < t_idx
    eye = (s_idx == t_idx).astype(F32)
    blk16 = (t_idx // 16) == (s_idx // 16)
    blk32 = (t_idx // 32) == (s_idx // 32)
    stack = functools.partial(_stack4, mask=stack_mask)
    unstack = functools.partial(_unstack4, mask=stack_mask)

    slabs = [slice(i * S, (i + 1) * S) for i in range(RWKV_WIDTH // S)]
    each = lambda fn, *lists: [fn(*args) for args in zip(*lists)]
    ones_bd = stack_mask.astype(F32)

    kk_g = each(lambda sl: kk_raw[:, sl], slabs)
    ssq = each(lambda x: _dot(x * x, ones_bd, HIGHEST), kk_g)
    kk_g = each(lambda x, s2: x / jnp.maximum(jnp.sqrt(s2), 1e-12), kk_g, ssq)
    b_g = each(lambda x, sl: x * a[:, sl], kk_g, slabs)
    rg = each(lambda sl: r[:, sl] * g_in[:, sl], slabs)
    kkg = each(lambda x, sl: x * g_prev[:, sl], kk_g, slabs)
    v_s = each(lambda sl: stack(v[:, sl]), slabs)
    kd_s = each(lambda sl: stack(k_mod[:, sl] * g_inv[:, sl]), slabs)
    bd_s = each(lambda x, sl: stack(x * g_inv[:, sl]), b_g, slabs)
    kdg = each(lambda sl: _bf(k_mod[:, sl] * g_tail[:, sl]), slabs)
    bdg = each(lambda x, sl: _bf(x * g_tail[:, sl]), b_g, slabs)

    lhs = each(lambda x, y: _bf(jnp.concatenate([x, y], axis=0)), kkg, rg)
    with_k = each(_dot_nt, lhs, kd_s)
    with_b = each(_dot_nt, lhs, bd_s)
    a_k = each(lambda x: jnp.where(strict, x[:L], 0.0), with_k)
    a_b = each(lambda x: jnp.where(strict, x[:L], 0.0), with_b)
    ar_k = each(lambda x: _bf(jnp.where(incl, x[L:], 0.0)), with_k)
    ar_b = each(lambda x: _bf(jnp.where(incl, x[L:], 0.0)), with_b)

    pw = each(lambda x: jnp.where(blk16, -x, 0.0), a_b)
    t_inv = each(lambda x: eye + x, pw)
    pw_s = each(stack, pw)
    for _ in range(3):
        pw = each(lambda x, xs: _dot(_bf(x), xs), pw, pw_s)
        pw_s = each(stack, pw)
        t_inv = each(lambda t, xs: t + _dot(_bf(t), xs), t_inv, pw_s)
    for off_mask in (blk32 & (~blk16), ~blk32):
        off_s = each(lambda x: stack(jnp.where(off_mask, x, 0.0)), a_b)
        left = each(lambda t, o: _bf(_dot(_bf(t), o)), t_inv, off_s)
        t_inv = each(lambda t, lt: t - _dot(lt, stack(t)), t_inv, left)
    t_b = each(_bf, t_inv)

    p = each(lambda t, x: _dot(t, stack(x)), t_b, kkg)
    akv = each(lambda x, vs: _dot(_bf(x), vs), a_k, v_s)
    q = each(lambda t, x: _dot(t, stack(x)), t_b, akv)
    for i, sl in enumerate(slabs):
        ry_ref[:, sl] = rg[i] - _dot(ar_b[i], stack(p[i]))
        y0_ref[:, sl] = _dot(ar_k[i], v_s[i]) - _dot(ar_b[i], stack(q[i]))
        m_ref[:, sl] = eye * g_last[:, sl] - unstack(_dot_tn(bdg[i], _bf(p[i])))
        n_ref[:, sl] = unstack(_dot_tn(kdg[i], _bf(v[:, sl])) - _dot_tn(bdg[i], _bf(q[i])))


def _head_block_mask(s):
    row = lax.broadcasted_iota(jnp.int32, (s, s), 0)
    col = lax.broadcasted_iota(jnp.int32, (s, s), 1)
    return (row // RWKV_HEAD_DIM) == (col // RWKV_HEAD_DIM)


def _stack4(x, mask, dtype=BF16):
    reps = mask.shape[0] // x.shape[0]
    return jnp.where(mask, jnp.concatenate([x] * reps, axis=0), 0.0).astype(dtype)


def _unstack4(x, mask):
    L = RWKV_HEAD_DIM
    x = jnp.where(mask, x, 0.0)
    return sum(x[i * L:(i + 1) * L] for i in range(x.shape[0] // L))


def _rwkv_scan_body(m_ref, n_ref, ry_ref, y0_ref, rkk_ref, v_ref, g_ref, lng_ref, lnb_ref, o_ref, h_ref):
    S = RWKV_GROUP
    mask = _head_block_mask(S)
    head_mean = mask.astype(F32) * (1.0 / RWKV_HEAD_DIM)

    @pl.when(pl.program_id(0) == 0)
    def _():
        h_ref[...] = jnp.zeros_like(h_ref)

    slabs = [slice(i * S, (i + 1) * S) for i in range(RWKV_WIDTH // S)]
    each = lambda fn, *lists: [fn(*args) for args in zip(*lists)]
    state = [_bf(h_ref[i]) for i in range(len(slabs))]
    y = each(lambda sl, h: _dot(_bf(ry_ref[:, sl]), h) + y0_ref[:, sl], slabs, state)
    for i, sl in enumerate(slabs):
        h_ref[i] = _dot(_stack4(m_ref[:, sl], mask), state[i]) + _stack4(n_ref[:, sl], mask, F32)
    mean = each(lambda x: _dot(x, head_mean, HIGHEST), y)
    var = each(lambda x, mu: _dot(jnp.square(x - mu), head_mean, HIGHEST), y, mean)
    rk = each(lambda sl: _dot(rkk_ref[:, sl], mask.astype(F32), HIGHEST), slabs)
    for i, sl in enumerate(slabs):
        yn = (y[i] - mean[i]) * lax.rsqrt(var[i] + GN_EPS) * lng_ref[:, sl] + lnb_ref[:, sl]
        o_ref[:, sl] = ((yn + rk[i] * v_ref[:, sl]) * g_ref[:, sl]).astype(o_ref.dtype)


def _rwkv7(proj, lora_col0, mu_rkv, mu_lora, w0, w_up, a0, a_up, g_up, k_k, k_a, r_k, ln_g, ln_b, out_dtype=F32):
    t = proj.shape[0]
    nc = t // CHUNK
    W, L = RWKV_WIDTH, CHUNK
    lo = lora_col0 // LORA_PAD
    go = (lora_col0 + 2 * LORA_PAD) // GATE_LORA
    row_w = lambda i: (i, 0)
    vec = lambda x: x.reshape(1, -1)
    pad_rows = lambda x: jnp.pad(x, ((0, LORA_PAD - x.shape[0]), (0, 0)))
    full = lambda shape: pl.BlockSpec(shape, lambda i: (0,) * len(shape))
    rows_per_prev = L // PREV_ROWS
    cur = lambda width, blk: pl.BlockSpec((L, width), lambda i: (i, blk))
    prev = lambda width, blk: pl.BlockSpec((PREV_ROWS, width), lambda i: (jnp.maximum(i * rows_per_prev - 1, 0), blk))
    columns = [(W, 0), (W, 1), (W, 2), (LORA_PAD, lo), (LORA_PAD, lo + 1), (GATE_LORA, go)]
    mus = [mu_rkv[:W], mu_rkv[W:2 * W], mu_rkv[2 * W:], mu_lora[:LORA_PAD], mu_lora[LORA_PAD:2 * LORA_PAD],
           mu_lora[2 * LORA_PAD:]]
    m, n, ry, y0, rkk, v_s, g = pl.pallas_call(
        _rwkv_chunk_body,
        grid=(nc,),
        in_specs=(
            [cur(w, b) for w, b in columns] + [prev(w, b) for w, b in columns]
            + [full((1, w)) for w, _ in columns]
            + [full((1, W)), full((LORA_PAD, W)), full((1, W)), full((LORA_PAD, W)), full((GATE_LORA, W)),
               full((1, W)), full((1, W)), full((1, W))]),
        out_specs=[pl.BlockSpec((L, W), row_w)] * 7,
        out_shape=[jax.ShapeDtypeStruct((t, W), F32)] * 7,
        compiler_params=_params("arbitrary"),
    )(*([proj] * 12), *[vec(mu) for mu in mus],
      vec(w0), pad_rows(w_up), vec(a0), pad_rows(a_up), g_up, vec(k_k), vec(k_a), vec(r_k))

    return pl.pallas_call(
        _rwkv_scan_body,
        grid=(nc,),
        in_specs=[pl.BlockSpec((L, W), row_w)] * 7 + [full((1, W)), full((1, W))],
        out_specs=pl.BlockSpec((L, W), row_w),
        out_shape=jax.ShapeDtypeStruct((t, W), out_dtype),
        scratch_shapes=[pltpu.VMEM((W // RWKV_GROUP, RWKV_GROUP, RWKV_GROUP), F32)],
        compiler_params=_params("arbitrary"),
    )(m, n, ry, y0, rkk, v_s, g, vec(ln_g), vec(ln_b))


def _mlstm_body(q_ref, k_ref, qp_ref, kp_ref, cw_ref, v_ref, o_ref, gates_ref, bif_ref, ng_ref, out_ref,
                c_ref, n_ref, m_ref):
    L, D = CHUNK, MLSTM_HEAD_DIM
    first_chunk = pl.program_id(0) == 0

    @pl.when(first_chunk)
    def _():
        c_ref[...] = jnp.zeros_like(c_ref)
        n_ref[...] = jnp.zeros_like(n_ref)
        m_ref[...] = jnp.zeros_like(m_ref)

    def conv_silu(x_ref, prev_ref, w):
        x = x_ref[...]
        before = jnp.where(first_chunk, 0.0, prev_ref[...])
        head_rows = lax.broadcasted_iota(jnp.int32, before.shape, 0)
        acc = x * w[MLSTM_CONV - 1:MLSTM_CONV, :]
        for d in range(1, MLSTM_CONV):
            rolled = pltpu.roll(x, d, axis=0)
            top = jnp.where(head_rows < d, pltpu.roll(before, d, axis=0), rolled[:PREV_ROWS])
            shifted = jnp.concatenate([top, rolled[PREV_ROWS:]], axis=0)
            acc = acc + shifted * w[MLSTM_CONV - 1 - d:MLSTM_CONV - d, :]
        return acc * jax.nn.sigmoid(acc)

    q_all = conv_silu(q_ref, qp_ref, cw_ref[:, :MLSTM_WIDTH])
    k_all = conv_silu(k_ref, kp_ref, cw_ref[:, MLSTM_WIDTH:])

    row = lax.broadcasted_iota(jnp.int32, (L, L), 0)
    col = lax.broadcasted_iota(jnp.int32, (L, L), 1)
    incl = col <= row
    eye = (col == row).astype(F32)
    gates = gates_ref[...] + bif_ref[...]
    b_all = _dot(incl.astype(F32), jax.nn.log_sigmoid(gates), HIGHEST)
    to_row = lambda c: jnp.sum(c * eye, axis=0, keepdims=True)

    for h in range(MLSTM_HEADS):
        sl = slice(h * D, (h + 1) * D)
        q = q_all[:, sl] * (D ** -0.5)
        k = k_all[:, sl]
        v = v_ref[:, sl]
        i_col = gates[:, h:h + 1]
        b_col = b_all[:, MLSTM_HEADS + h:MLSTM_HEADS + h + 1]
        m_prev = m_ref[h]
        c_mat = c_ref[h]
        n_vec = n_ref[h]
        d_mat = jnp.where(incl, b_col - to_row(b_col) + to_row(i_col), -jnp.inf)
        inter = b_col + m_prev
        m_row = jnp.maximum(inter, jnp.max(d_mat, axis=-1, keepdims=True))
        w_inter = jnp.exp(inter - m_row)
        q_b, k_b = _bf(q), _bf(k)
        s = _dot_nt(q_b, k_b) * jnp.exp(d_mat - m_row)
        num = _dot(_bf(s), _bf(v)) + w_inter * _dot_nt(q_b, _bf(c_mat))
        den = jnp.sum(s, axis=-1, keepdims=True) + w_inter * jnp.sum(q * n_vec, axis=-1, keepdims=True)
        hh = num / jnp.maximum(jnp.abs(den), jnp.exp(-m_row))
        b_last = b_col[L - 1:L, :]
        d_state = b_last - b_col + i_col
        m_new = jnp.maximum(b_last + m_prev, jnp.max(d_state, axis=0, keepdims=True))
        w_state = jnp.exp(d_state - m_new)
        w_carry = jnp.exp(b_last + m_prev - m_new)
        c_ref[h] = w_carry * c_mat + _dot_tn(_bf(w_state * v), k_b)
        n_ref[h] = w_carry * n_vec + jnp.sum(w_state * k, axis=0, keepdims=True)
        m_ref[h] = m_new
        hn = hh * lax.rsqrt(jnp.mean(hh * hh, axis=-1, keepdims=True) + NORM_EPS) * ng_ref[:, sl]
        out_ref[:, sl] = (hn * jax.nn.sigmoid(o_ref[:, sl])).astype(out_ref.dtype)


def _mlstm(proj, q_blk, v_blk, o_blk, gates_blk, conv_w, b_if, norm_g, out_dtype=F32):
    t = proj.shape[0]
    L, W = CHUNK, MLSTM_WIDTH
    bif = jnp.pad(b_if, (0, LORA_PAD - b_if.shape[0])).reshape(1, LORA_PAD)
    prev = lambda blk: pl.BlockSpec((PREV_ROWS, W), lambda i: (jnp.maximum(i * (L // PREV_ROWS) - 1, 0), blk))
    return pl.pallas_call(
        _mlstm_body,
        grid=(t // L,),
        in_specs=[
            pl.BlockSpec((L, W), lambda i: (i, q_blk)),
            pl.BlockSpec((L, W), lambda i: (i, q_blk + 1)),
            prev(q_blk), prev(q_blk + 1),
            pl.BlockSpec((MLSTM_CONV, 2 * W), lambda i: (0, 0)),
            pl.BlockSpec((L, W), lambda i: (i, v_blk)),
            pl.BlockSpec((L, W), lambda i: (i, o_blk)),
            pl.BlockSpec((L, LORA_PAD), lambda i: (i, gates_blk)),
            pl.BlockSpec((1, LORA_PAD), lambda i: (0, 0)),
            pl.BlockSpec((1, W), lambda i: (0, 0)),
        ],
        out_specs=pl.BlockSpec((L, W), lambda i: (i, 0)),
        out_shape=jax.ShapeDtypeStruct((t, W), out_dtype),
        scratch_shapes=[pltpu.VMEM((MLSTM_HEADS, MLSTM_HEAD_DIM, MLSTM_HEAD_DIM), F32),
                        pltpu.VMEM((MLSTM_HEADS, 1, MLSTM_HEAD_DIM), F32),
                        pltpu.VMEM((MLSTM_HEADS, 1, 1), F32)],
        compiler_params=_params("arbitrary"),
    )(proj, proj, proj, proj, conv_w, proj, proj, proj, bif, norm_g.reshape(1, W))


def _fox_qkv_prep_body(x_ref, g_ref, o_ref):
    D = FOX_HEAD_DIM

    @pl.when(pl.program_id(1) < 2)
    def _():
        for h in range(FOX_HEADS):
            sl = slice(h * D, (h + 1) * D)
            x = x_ref[:, sl]
            y = x * lax.rsqrt(jnp.mean(x * x, axis=-1, keepdims=True) + NORM_EPS) * g_ref[...]
            o_ref[:, sl] = y.astype(o_ref.dtype)

    @pl.when(pl.program_id(1) == 2)
    def _():
        o_ref[...] = x_ref[...].astype(o_ref.dtype)


def _fox_qkv_prep(proj, qn_g, kn_g, tm=512):
    t = proj.shape[0]
    W = FOX_HEADS * FOX_HEAD_DIM
    gains = jnp.stack([qn_g * (FOX_HEAD_DIM ** -0.5 * LOG2E), kn_g, jnp.ones_like(kn_g)]).reshape(3, 1, FOX_HEAD_DIM)
    return pl.pallas_call(
        _fox_qkv_prep_body,
        grid=(t // tm, 3),
        in_specs=[pl.BlockSpec((tm, W), lambda i, j: (i, j)),
                  pl.BlockSpec((None, 1, FOX_HEAD_DIM), lambda i, j: (j, 0, 0))],
        out_specs=pl.BlockSpec((tm, W), lambda i, j: (i, j)),
        out_shape=jax.ShapeDtypeStruct((t, 3 * W), BF16),
        compiler_params=_params("arbitrary", "arbitrary"),
    )(proj, gains)


def _log_forget_cumsum_body(f_ref, b_ref, o_ref, carry_ref):
    tm = f_ref.shape[0]

    @pl.when(pl.program_id(0) == 0)
    def _():
        carry_ref[...] = jnp.zeros_like(carry_ref)

    row = lax.broadcasted_iota(jnp.int32, (tm, tm), 0)
    col = lax.broadcasted_iota(jnp.int32, (tm, tm), 1)
    log_f = jax.nn.log_sigmoid(f_ref[...] + b_ref[...])
    cum = _dot((col <= row).astype(F32), log_f, HIGHEST) + carry_ref[...]
    o_ref[...] = cum * LOG2E
    carry_ref[...] = cum[tm - 1:tm, :]


def _log_forget_cumsum(f_pre, b_f, tm=512):
    t, nh = f_pre.shape
    return pl.pallas_call(
        _log_forget_cumsum_body,
        grid=(t // tm,),
        in_specs=[pl.BlockSpec((tm, nh), lambda i: (i, 0)), pl.BlockSpec((1, nh), lambda i: (0, 0))],
        out_specs=pl.BlockSpec((tm, nh), lambda i: (i, 0)),
        out_shape=jax.ShapeDtypeStruct((t, nh), F32),
        scratch_shapes=[pltpu.VMEM((1, nh), F32)],
        compiler_params=_params("arbitrary"),
    )(f_pre, b_f.reshape(1, nh))


def _fox_body(q_ref, k_ref, v_ref, g_ref, cq_ref, ck_ref, o_ref, sa_ref, sb_ref, *, tq, tk, ts):
    qi = pl.program_id(1)
    n_sub = tq // ts
    subs = range(n_sub)
    rows = [pl.ds(i * ts, ts) for i in subs]

    def fill(dst_ref, kb):
        start = pl.multiple_of(kb * tk, tk)
        for i in subs:
            dst_ref[rows[i], :] = _dot_nt(q_ref[rows[i], :], k_ref[pl.ds(start, tk), :])

    def absorb(src_ref, kb, carry, diagonal):
        start = pl.multiple_of(kb * tk, tk)
        out = []
        for i in subs:
            m_prev, l_prev, acc = carry[i]
            width = (i + 1) * ts if diagonal else tk
            s = src_ref[rows[i], :width] + cq_ref[rows[i], :] - ck_ref[kb][:, :width]
            if diagonal:
                row = i * ts + lax.broadcasted_iota(jnp.int32, (ts, width), 0)
                col = lax.broadcasted_iota(jnp.int32, (ts, width), 1)
                s = jnp.where(col <= row, s, -jnp.inf)
            m_new = jnp.maximum(m_prev, jnp.max(s, axis=-1, keepdims=True))
            p = jnp.exp2(s - m_new)
            alpha = jnp.exp2(m_prev - m_new)
            l_new = alpha * l_prev + jnp.sum(p, axis=-1, keepdims=True)
            acc = alpha * acc + _dot(_bf(p), v_ref[pl.ds(start, width), :])
            out.append((m_new, l_new, acc))
        return tuple(out)

    def pair(pi, carry):
        fill(sb_ref, 2 * pi + 1)
        carry = absorb(sa_ref, 2 * pi, carry, False)
        fill(sa_ref, 2 * pi + 2)
        return absorb(sb_ref, 2 * pi + 1, carry, False)

    init = tuple((jnp.full((ts, 1), -jnp.inf, F32), jnp.zeros((ts, 1), F32), jnp.zeros((ts, FOX_HEAD_DIM), F32))
                 for _ in range(n_sub))
    fill(sa_ref, 0)
    carry = lax.fori_loop(0, qi // 2, pair, init)

    def odd_tail(carry):
        fill(sb_ref, qi)
        carry = absorb(sa_ref, qi - 1, carry, False)
        return absorb(sb_ref, qi, carry, True)

    final = lax.cond(qi % 2 == 1, odd_tail, lambda c: absorb(sa_ref, qi, c, True), carry)
    for i in range(n_sub):
        _, l_fin, acc = final[i]
        rows = pl.ds(i * ts, ts)
        o_ref[rows, :] = (acc / l_fin * jax.nn.sigmoid(g_ref[rows, :])).astype(o_ref.dtype)


def _fox_attention(qkv, proj, g_blk0, cum, tq=512):
    t = qkv.shape[0]
    D, H = FOX_HEAD_DIM, FOX_HEADS
    tk = tq
    cum_t = cum.T
    cq = cum_t.reshape(H, t, 1)
    ck = cum_t.reshape(H, t // tk, 1, tk)
    return pl.pallas_call(
        functools.partial(_fox_body, tq=tq, tk=tk, ts=FOX_SUB_TILE),
        grid=(H, t // tq),
        in_specs=[
            pl.BlockSpec((tq, D), lambda h, qi: (qi, h)),
            pl.BlockSpec((t, D), lambda h, qi: (0, H + h)),
            pl.BlockSpec((t, D), lambda h, qi: (0, 2 * H + h)),
            pl.BlockSpec((tq, D), lambda h, qi: (qi, g_blk0 + h)),
            pl.BlockSpec((None, tq, 1), lambda h, qi: (h, qi, 0)),
            pl.BlockSpec((None, t // tk, 1, tk), lambda h, qi: (h, 0, 0, 0)),
        ],
        out_specs=pl.BlockSpec((tq, D), lambda h, qi: (qi, h)),
        out_shape=jax.ShapeDtypeStruct((t, H * D), BF16),
        scratch_shapes=[pltpu.VMEM((tq, tk), F32), pltpu.VMEM((tq, tk), F32)],
        compiler_params=_params("arbitrary", "arbitrary"),
    )(qkv, qkv, qkv, proj, cq, ck)


def _xattn_body(q_ref, k_ref, v_ref, o_ref):
    s = _dot_nt(q_ref[...], k_ref[...]) * (XATTN_HEAD_DIM ** -0.5)
    p = jnp.exp(s - jnp.max(s, axis=-1, keepdims=True))
    o = _dot(p.astype(BF16), v_ref[...]) / jnp.sum(p, axis=-1, keepdims=True)
    o_ref[...] = o.astype(o_ref.dtype)


def _cross_attention(q, kv, tq=512):
    t = q.shape[0]
    n_mem = kv.shape[0]
    D, H = XATTN_HEAD_DIM, XATTN_HEADS
    return pl.pallas_call(
        _xattn_body,
        grid=(t // tq, H),
        in_specs=[pl.BlockSpec((tq, D), lambda i, h: (i, h)),
                  pl.BlockSpec((n_mem, D), lambda i, h: (0, h)),
                  pl.BlockSpec((n_mem, D), lambda i, h: (0, H + h))],
        out_specs=pl.BlockSpec((tq, D), lambda i, h: (i, h)),
        out_shape=jax.ShapeDtypeStruct((t, H * D), BF16),
        compiler_params=_params("arbitrary", "arbitrary"),
    )(q, kv, kv)


def _swiglu_up_body(x_ref, g_ref, wg_ref, wu_ref, o_ref, xn_ref):
    @pl.when(pl.program_id(1) == 0)
    def _():
        x = x_ref[...]
        xn_ref[...] = _bf(x * lax.rsqrt(jnp.mean(x * x, axis=-1, keepdims=True) + NORM_EPS) * g_ref[...])

    xn = xn_ref[...]
    a = _dot(xn, _bf(wg_ref[...]))
    b = _dot(xn, _bf(wu_ref[...]))
    o_ref[...] = (a * jax.nn.sigmoid(a) * b).astype(o_ref.dtype)


def _swiglu_up(x, norm_g, w_gate, w_up, tm=MM_ROW_TILE, tf=MM_COL_TILE):
    t, d = x.shape
    f = w_gate.shape[1]
    tm = min(tm, t)
    return pl.pallas_call(
        _swiglu_up_body,
        grid=(t // tm, f // tf),
        in_specs=[pl.BlockSpec((tm, d), lambda i, j: (i, 0)),
                  pl.BlockSpec((1, d), lambda i, j: (0, 0)),
                  pl.BlockSpec((d, tf), lambda i, j: (0, j)),
                  pl.BlockSpec((d, tf), lambda i, j: (0, j))],
        out_specs=pl.BlockSpec((tm, tf), lambda i, j: (i, j)),
        out_shape=jax.ShapeDtypeStruct((t, f), BF16),
        scratch_shapes=[pltpu.VMEM((tm, d), BF16)],
        compiler_params=_params("arbitrary", "arbitrary"),
    )(x, norm_g.reshape(1, d), w_gate, w_up)


def _mm_acc_body(x_ref, w_ref, res_ref, o_ref):
    @pl.when(pl.program_id(2) == 0)
    def _():
        o_ref[...] = res_ref[...]

    o_ref[...] += _dot(x_ref[...], _bf(w_ref[...]))


def _mm_acc(x, w, res, tm=MM_ROW_TILE, tn=MM_ROW_TILE, tk=MM_COL_TILE):
    t, k = x.shape
    n = w.shape[1]
    tm = min(tm, t)
    return pl.pallas_call(
        _mm_acc_body,
        grid=(t // tm, n // tn, k // tk),
        in_specs=[pl.BlockSpec((tm, tk), lambda i, j, kk: (i, kk)),
                  pl.BlockSpec((tk, tn), lambda i, j, kk: (kk, j)),
                  pl.BlockSpec((tm, tn), lambda i, j, kk: (i, j))],
        out_specs=pl.BlockSpec((tm, tn), lambda i, j, kk: (i, j)),
        out_shape=jax.ShapeDtypeStruct((t, n), F32),
        compiler_params=_params("arbitrary", "arbitrary", "arbitrary"),
    )(x, w, res)


def _rmsnorm_body(x_ref, g_ref, o_ref):
    x = x_ref[...]
    o_ref[...] = x * lax.rsqrt(jnp.mean(x * x, axis=-1, keepdims=True) + NORM_EPS) * g_ref[...]


def _rmsnorm(x, g, tm=512):
    t, d = x.shape
    return pl.pallas_call(
        _rmsnorm_body,
        grid=(t // tm,),
        in_specs=[pl.BlockSpec((tm, d), lambda i: (i, 0)), pl.BlockSpec((1, d), lambda i: (0, 0))],
        out_specs=pl.BlockSpec((tm, d), lambda i: (i, 0)),
        out_shape=jax.ShapeDtypeStruct((t, d), F32),
        compiler_params=_params("arbitrary"),
    )(x, g.reshape(1, d))


def _pad_cols(w, n=LORA_PAD):
    return jnp.pad(w, [(0, 0)] * (w.ndim - 1) + [(0, n - w.shape[-1])])


def _pack_ab_columns(w):
    a_cols = 3 * RWKV_WIDTH + DECAY_LORA + AAA_LORA + GATE_LORA
    a, b = w[..., :a_cols], w[..., a_cols:]
    o1 = 3 * RWKV_WIDTH
    o2 = o1 + DECAY_LORA
    o3 = o2 + AAA_LORA
    return jnp.concatenate([
        a[..., :o1], b[..., :4 * MLSTM_WIDTH],
        _pad_cols(a[..., o1:o2]), _pad_cols(a[..., o2:o3]), a[..., o3:],
        _pad_cols(b[..., 4 * MLSTM_WIDTH:])], axis=-1)


def _rwkv_mlstm_mixer(x, norm_g, w_in, mu, w0, w_up, a0, a_up, g_up, k_k, k_a, r_k, ln_g, ln_b,
                      conv_qk, b_if, mh_g, w_out):
    proj = _norm_mm(x, _pack_ab_columns(w_in), norm_g=norm_g)
    mu_p = _pack_ab_columns(jnp.concatenate([mu, jnp.zeros((4 * MLSTM_WIDTH + 2 * MLSTM_HEADS,), F32)]))
    o1 = 3 * RWKV_WIDTH
    o_lora = o1 + 4 * MLSTM_WIDTH
    lora_w = 2 * LORA_PAD + GATE_LORA
    y_a = _rwkv7(proj, o_lora, mu_p[:o1], mu_p[o_lora:o_lora + lora_w], w0, w_up, a0, a_up, g_up, k_k, k_a, r_k,
                 ln_g, ln_b, out_dtype=BF16)
    y_b = _mlstm(proj, o1 // MLSTM_WIDTH, o1 // MLSTM_WIDTH + 2, o1 // MLSTM_WIDTH + 3,
                 (o_lora + lora_w) // LORA_PAD, conv_qk, b_if, mh_g, out_dtype=BF16)

    x = _norm_mm(y_a, w_out[:RWKV_WIDTH], res=x)
    return _norm_mm(y_b, w_out[RWKV_WIDTH:], res=x)


def _fox_layer(x, norm_g, w_in, b_f, qn_g, kn_g, w_out):
    W = FOX_HEADS * FOX_HEAD_DIM
    proj = _norm_mm(x, w_in, norm_g=norm_g)
    qkv = _fox_qkv_prep(proj, qn_g, kn_g)
    cum = _log_forget_cumsum(proj[:, 4 * W:], b_f)
    o = _fox_attention(qkv, proj, 3 * FOX_HEADS, cum)
    return _norm_mm(o, w_out, res=x)


def _xattn_layer(x, norm_g, kv, wq, wo):
    q = _norm_mm(x, wq, norm_g=norm_g, out_dtype=BF16)
    return _norm_mm(_cross_attention(q, kv), wo, res=x)


def _moe_experts_body(tile_e_ref, n_valid_ref, x_ref, g_ref, wg_ref, wu_ref, wd_ref, gate_ref, o_ref, xn_ref):
    i = pl.program_id(0)
    j = pl.program_id(1)

    @pl.when(j == 0)
    def _():
        o_ref[...] = jnp.zeros_like(o_ref)
        x = x_ref[...]
        xn_ref[...] = _bf(x * lax.rsqrt(jnp.mean(x * x, axis=-1, keepdims=True) + NORM_EPS) * g_ref[...])

    @pl.when(i < n_valid_ref[0])
    def _():
        x = xn_ref[...]
        a = _dot(x, _bf(wg_ref[...]))
        b = _dot(x, _bf(wu_ref[...]))
        h = a * jax.nn.sigmoid(a) * b
        o_ref[...] += _dot(_bf(h), _bf(wd_ref[...]))

    @pl.when(j == pl.num_programs(1) - 1)
    def _():
        o_ref[...] = o_ref[...] * gate_ref[...]


def _moe_experts(xg, norm_g, gate_at, tile_e, n_valid, w_gate, w_up, w_down, tm, tf=256):
    p, d = xg.shape
    f = w_gate.shape[2]
    nf = f // tf

    def expert(i, te, nv):
        return te[jnp.minimum(i, nv[0] - 1)]

    def fblock(i, j, nv):
        return jnp.where(i < nv[0], j, nf - 1)

    grid_spec = pltpu.PrefetchScalarGridSpec(
        num_scalar_prefetch=2,
        grid=(p // tm, nf),
        in_specs=[
            pl.BlockSpec((tm, d), lambda i, j, te, nv: (i, 0), pipeline_mode=pl.Buffered(1)),
            pl.BlockSpec((1, d), lambda i, j, te, nv: (0, 0)),
            pl.BlockSpec((None, d, tf), lambda i, j, te, nv: (expert(i, te, nv), 0, fblock(i, j, nv))),
            pl.BlockSpec((None, d, tf), lambda i, j, te, nv: (expert(i, te, nv), 0, fblock(i, j, nv))),
            pl.BlockSpec((None, tf, d), lambda i, j, te, nv: (expert(i, te, nv), fblock(i, j, nv), 0)),
            pl.BlockSpec((tm, 1), lambda i, j, te, nv: (i, 0)),
        ],
        out_specs=pl.BlockSpec((tm, d), lambda i, j, te, nv: (i, 0)),
        scratch_shapes=[pltpu.VMEM((tm, d), BF16)],
    )
    return pl.pallas_call(
        _moe_experts_body,
        grid_spec=grid_spec,
        out_shape=jax.ShapeDtypeStruct((p, d), F32),
        compiler_params=_params("arbitrary", "arbitrary"),
    )(tile_e, n_valid, xg, norm_g.reshape(1, d), w_gate, w_up, w_down, gate_at)


MOE_TILE = 1024


def _moe_layer(x, norm_g, w_router, w_gate, w_up, w_down):
    t = x.shape[0]
    tm = MOE_TILE
    logits = _norm_mm(x, _pad_cols(w_router), norm_g=norm_g, tm=512, mxu_dtype=F32, precision=HIGHEST)
    top_val, top_idx = lax.top_k(logits[:, :N_EXPERTS], TOP_K)
    gates = jax.nn.softmax(top_val, axis=-1)
    flat_e = top_idx.reshape(-1)
    onehot = (flat_e[:, None] == jnp.arange(N_EXPERTS)[None, :]).astype(jnp.int32)
    ranks = jnp.cumsum(onehot, axis=0) - onehot
    counts = jnp.sum(onehot, axis=0)
    padded = (counts + tm - 1) // tm * tm
    pad_end = jnp.cumsum(padded)
    dest = (pad_end - padded)[flat_e] + jnp.sum(ranks * onehot, axis=1)
    n_rows = TOP_K * t + N_EXPERTS * tm
    tok_at = (jnp.arange(n_rows, dtype=jnp.int32) % t).at[dest].set(jnp.arange(TOP_K * t, dtype=jnp.int32) // TOP_K)
    gate_at = jnp.zeros((n_rows,), F32).at[dest].set(gates.reshape(-1))
    tile_start = jnp.arange(n_rows // tm, dtype=jnp.int32) * tm
    tile_e = jnp.minimum(jnp.sum((tile_start[:, None] >= pad_end[None, :]).astype(jnp.int32), axis=1), N_EXPERTS - 1)
    n_valid = (pad_end[-1:] // tm).astype(jnp.int32)
    xg = jnp.take(x, tok_at, axis=0)
    y = _moe_experts(xg, norm_g, gate_at.reshape(n_rows, 1), tile_e, n_valid, w_gate, w_up, w_down, tm)
    dest2 = dest.reshape(t, TOP_K)
    return x + jnp.take(y, dest2[:, 0], axis=0) + jnp.take(y, dest2[:, 1], axis=0)


def kernel(x, mem, mem_norm_g, final_norm_g, norm_mix_g, norm_xattn_g, norm_ffn_g,
           xa_wq, xa_wkv, xa_wo,
           ab_w_in, ab_mu, rw_w0, rw_w_up, rw_a0, rw_a_up, rw_g_up, rw_k_k, rw_k_a, rw_r_k,
           rw_ln_g, rw_ln_b, ml_conv, ml_b_if, ml_norm_g, ab_w_out,
           ffn_w_gate, ffn_w_up, ffn_w_down,
           fox_w_in, fox_b_f, fox_qn_g, fox_kn_g, fox_w_out,
           moe_router, moe_w_gate, moe_w_up, moe_w_down):
    bsz, seq, d = x.shape
    assert bsz == 1 and d == D_MODEL and seq % 512 == 0
    xs = x[0]
    mem_s = mem[0]
    depth = norm_mix_g.shape[0]
    for layer in range(depth):
        j = layer // 2
        if layer % 2 == 0:
            xs = _rwkv_mlstm_mixer(xs, norm_mix_g[layer], ab_w_in[j], ab_mu[j], rw_w0[j], rw_w_up[j], rw_a0[j],
                                   rw_a_up[j], rw_g_up[j], rw_k_k[j], rw_k_a[j], rw_r_k[j], rw_ln_g[j],
                                   rw_ln_b[j], ml_conv[j], ml_b_if[j], ml_norm_g[j], ab_w_out[j])
        else:
            xs = _fox_layer(xs, norm_mix_g[layer], fox_w_in[j], fox_b_f[j], fox_qn_g[j], fox_kn_g[j],
                            fox_w_out[j])
        kv = _norm_mm(mem_s, xa_wkv[layer], norm_g=mem_norm_g, tm=mem_s.shape[0], out_dtype=BF16)
        xs = _xattn_layer(xs, norm_xattn_g[layer], kv, xa_wq[layer], xa_wo[layer])
        if layer % 2 == 0:
            hidden = _swiglu_up(xs, norm_ffn_g[layer], ffn_w_gate[j], ffn_w_up[j])
            xs = _mm_acc(hidden, ffn_w_down[j], xs)
        else:
            xs = _moe_layer(xs, norm_ffn_g[layer], moe_router[j], moe_w_gate[j], moe_w_up[j], moe_w_down[j])
    return _rmsnorm(xs, final_norm_g)[None]
```

```python
import functools

import jax
import jax.numpy as jnp
from jax import lax
from jax.experimental import pallas as pl
from jax.experimental.pallas import tpu as pltpu

F32 = jnp.float32
BF16 = jnp.bfloat16
HIGHEST = lax.Precision.HIGHEST

D_MODEL = 2048
CHUNK = 64
NORM_EPS = 1e-6
GN_EPS = 64e-5
RWKV_WIDTH = 1024
RWKV_HEAD_DIM = 64
RWKV_HEADS = 16
RWKV_GROUP = 256
DECAY_LORA = 96
AAA_LORA = 96
GATE_LORA = 256
PREV_ROWS = 8
LORA_PAD = 128
MLSTM_WIDTH = 1024
MLSTM_HEAD_DIM = 256
MLSTM_HEADS = 4
MLSTM_CONV = 4
FOX_HEAD_DIM = 128
FOX_HEADS = 16
XATTN_HEADS = 4
XATTN_HEAD_DIM = 512
D_FF = 5632
N_EXPERTS = 8
TOP_K = 2

VMEM_LIMIT_BYTES = 56 * 1024 * 1024
LOG2E = 1.4426950408889634
FOX_SUB_TILE = 128
MM_ROW_TILE = 1024
MM_COL_TILE = 512


def _params(*semantics):
    return pltpu.CompilerParams(dimension_semantics=semantics, vmem_limit_bytes=VMEM_LIMIT_BYTES)


def _bf(a):
    return a.astype(BF16)


def _dot(a, b, precision=None):
    return jnp.dot(a, b, preferred_element_type=F32, precision=precision)


def _dot_nt(a, b, precision=None):
    return lax.dot_general(a, b, (((1,), (1,)), ((), ())), preferred_element_type=F32, precision=precision)


def _dot_tn(a, b, precision=None):
    return lax.dot_general(a, b, (((0,), (0,)), ((), ())), preferred_element_type=F32, precision=precision)


def _norm_mm_body(*refs, has_norm, has_res, precision):
    it = iter(refs)
    x_ref = next(it)
    g_ref = next(it) if has_norm else None
    w_ref = next(it)
    res_ref = next(it) if has_res else None
    o_ref = next(it)
    xn_ref = next(it)

    @pl.when(pl.program_id(1) == 0)
    def _():
        x = x_ref[...].astype(F32)
        if has_norm:
            x = x * lax.rsqrt(jnp.mean(x * x, axis=-1, keepdims=True) + NORM_EPS) * g_ref[...]
        xn_ref[...] = x.astype(xn_ref.dtype)

    acc = _dot(xn_ref[...], w_ref[...].astype(xn_ref.dtype), precision)
    if has_res:
        acc = acc + res_ref[...]
    o_ref[...] = acc.astype(o_ref.dtype)


def _norm_mm(x, w, *, norm_g=None, res=None, tm=MM_ROW_TILE, tn=MM_COL_TILE, out_dtype=F32, mxu_dtype=BF16,
             precision=None):
    m, k = x.shape
    n = w.shape[1]
    tm, tn = min(tm, m), min(tn, n)
    assert m % tm == 0
    in_specs = [pl.BlockSpec((tm, k), lambda i, j: (i, 0))]
    args = [x]
    if norm_g is not None:
        in_specs.append(pl.BlockSpec((1, k), lambda i, j: (0, 0)))
        args.append(norm_g.reshape(1, k))
    in_specs.append(pl.BlockSpec((k, tn), lambda i, j: (0, j)))
    args.append(w)
    if res is not None:
        in_specs.append(pl.BlockSpec((tm, tn), lambda i, j: (i, j)))
        args.append(res)
    return pl.pallas_call(
        functools.partial(_norm_mm_body, has_norm=norm_g is not None, has_res=res is not None, precision=precision),
        grid=(m // tm, pl.cdiv(n, tn)),
        in_specs=in_specs,
        out_specs=pl.BlockSpec((tm, tn), lambda i, j: (i, j)),
        out_shape=jax.ShapeDtypeStruct((m, n), out_dtype),
        scratch_shapes=[pltpu.VMEM((tm, k), mxu_dtype)],
        compiler_params=_params("arbitrary", "arbitrary"),
    )(*args)


def _rwkv_chunk_body(r_ref, k_ref, v_ref, wlo_ref, alo_ref, glo_ref,
                     rp_ref, kp_ref, vp_ref, wlop_ref, alop_ref, glop_ref,
                     mur_ref, muk_ref, muv_ref, muw_ref, mua_ref, mug_ref,
                     w0_ref, wup_ref, a0_ref, aup_ref, gup_ref, kk_ref, ka_ref, rk_ref,
                     m_ref, n_ref, ry_ref, y0_ref, rkk_ref, vs_ref, g_ref):
    L, N = CHUNK, RWKV_HEAD_DIM
    first_chunk = pl.program_id(0) == 0

    def token_shift(x_ref, prev_ref, mu_ref):
        x = x_ref[...]
        before = jnp.where(first_chunk, 0.0, prev_ref[PREV_ROWS - 1:PREV_ROWS, :])
        first_row = lax.broadcasted_iota(jnp.int32, x.shape, 0) == 0
        prev = jnp.where(first_row, before, pltpu.roll(x, 1, axis=0))
        return x + (prev - x) * mu_ref[...]

    r = token_shift(r_ref, rp_ref, mur_ref)
    k = token_shift(k_ref, kp_ref, muk_ref)
    v = token_shift(v_ref, vp_ref, muv_ref)
    w_lo = token_shift(wlo_ref, wlop_ref, muw_ref)
    a_lo = token_shift(alo_ref, alop_ref, mua_ref)
    g_lo = token_shift(glo_ref, glop_ref, mug_ref)
    log_w = -jax.nn.softplus(-(w0_ref[...] + _dot(jnp.tanh(w_lo), wup_ref[...], HIGHEST))) - 0.5
    lw = -jnp.exp(log_w)
    a = jax.nn.sigmoid(a0_ref[...] + _dot(a_lo, aup_ref[...], HIGHEST))
    g_ref[...] = _dot(jax.nn.sigmoid(g_lo), gup_ref[...], HIGHEST)
    k_mod = k * (1.0 + (a - 1.0) * ka_ref[...])
    rkk_ref[...] = r * k_mod * rk_ref[...]
    vs_ref[...] = v
    kk_raw = k * kk_ref[...]

    row = lax.broadcasted_iota(jnp.int32, (L, L), 0)
    col = lax.broadcasted_iota(jnp.int32, (L, L), 1)
    cum = _dot((col <= row).astype(F32), lw, HIGHEST)
    cum_last = cum[L - 1:L, :]
    g_in = jnp.exp(cum)
    g_prev = jnp.exp(cum - lw)
    g_inv = jnp.exp(-cum)
    g_tail = jnp.exp(cum_last - cum)
    g_last = jnp.exp(cum_last)

    S = RWKV_GROUP
    stack_mask = _head_block_mask(S)
    t_idx = lax.broadcasted_iota(jnp.int32, (L, S), 0)
    s_idx = lax.broadcasted_iota(jnp.int32, (L, S), 1) % N
    incl = s_idx <= t_idx
    strict = s_idx < t_idx
    eye = (s_idx == t_idx).astype(F32)
    blk16 = (t_idx // 16) == (s_idx // 16)
    blk32 = (t_idx // 32) == (s_idx // 32)
    stack = functools.partial(_stack4, mask=stack_mask)
    unstack = functools.partial(_unstack4, mask=stack_mask)

    slabs = [slice(i * S, (i + 1) * S) for i in range(RWKV_WIDTH // S)]
    each = lambda fn, *lists: [fn(*args) for args in zip(*lists)]
    ones_bd = stack_mask.astype(F32)

    kk_g = each(lambda sl: kk_raw[:, sl], slabs)
    ssq = each(lambda x: _dot(x * x, ones_bd, HIGHEST), kk_g)
    kk_g = each(lambda x, s2: x / jnp.maximum(jnp.sqrt(s2), 1e-12), kk_g, ssq)
    b_g = each(lambda x, sl: x * a[:, sl], kk_g, slabs)
    rg = each(lambda sl: r[:, sl] * g_in[:, sl], slabs)
    kkg = each(lambda x, sl: x * g_prev[:, sl], kk_g, slabs)
    v_s = each(lambda sl: stack(v[:, sl]), slabs)
    kd_s = each(lambda sl: stack(k_mod[:, sl] * g_inv[:, sl]), slabs)
    bd_s = each(lambda x, sl: stack(x * g_inv[:, sl]), b_g, slabs)
    kdg = each(lambda sl: _bf(k_mod[:, sl] * g_tail[:, sl]), slabs)
    bdg = each(lambda x, sl: _bf(x * g_tail[:, sl]), b_g, slabs)

    lhs = each(lambda x, y: _bf(jnp.concatenate([x, y], axis=0)), kkg, rg)
    with_k = each(_dot_nt, lhs, kd_s)
    with_b = each(_dot_nt, lhs, bd_s)
    a_k = each(lambda x: jnp.where(strict, x[:L], 0.0), with_k)
    a_b = each(lambda x: jnp.where(strict, x[:L], 0.0), with_b)
    ar_k = each(lambda x: _bf(jnp.where(incl, x[L:], 0.0)), with_k)
    ar_b = each(lambda x: _bf(jnp.where(incl, x[L:], 0.0)), with_b)

    pw = each(lambda x: jnp.where(blk16, -x, 0.0), a_b)
    t_inv = each(lambda x: eye + x, pw)
    pw_s = each(stack, pw)
    for _ in range(3):
        pw = each(lambda x, xs: _dot(_bf(x), xs), pw, pw_s)
        pw_s = each(stack, pw)
        t_inv = each(lambda t, xs: t + _dot(_bf(t), xs), t_inv, pw_s)
    for off_mask in (blk32 & (~blk16), ~blk32):
        off_s = each(lambda x: stack(jnp.where(off_mask, x, 0.0)), a_b)
        left = each(lambda t, o: _bf(_dot(_bf(t), o)), t_inv, off_s)
        t_inv = each(lambda t, lt: t - _dot(lt, stack(t)), t_inv, left)
    t_b = each(_bf, t_inv)

    p = each(lambda t, x: _dot(t, stack(x)), t_b, kkg)
    akv = each(lambda x, vs: _dot(_bf(x), vs), a_k, v_s)
    q = each(lambda t, x: _dot(t, stack(x)), t_b, akv)
    for i, sl in enumerate(slabs):
        ry_ref[:, sl] = rg[i] - _dot(ar_b[i], stack(p[i]))
        y0_ref[:, sl] = _dot(ar_k[i], v_s[i]) - _dot(ar_b[i], stack(q[i]))
        m_ref[:, sl] = eye * g_last[:, sl] - unstack(_dot_tn(bdg[i], _bf(p[i])))
        n_ref[:, sl] = unstack(_dot_tn(kdg[i], _bf(v[:, sl])) - _dot_tn(bdg[i], _bf(q[i])))


def _head_block_mask(s):
    row = lax.broadcasted_iota(jnp.int32, (s, s), 0)
    col = lax.broadcasted_iota(jnp.int32, (s, s), 1)
    return (row // RWKV_HEAD_DIM) == (col // RWKV_HEAD_DIM)


def _stack4(x, mask, dtype=BF16):
    reps = mask.shape[0] // x.shape[0]
    return jnp.where(mask, jnp.concatenate([x] * reps, axis=0), 0.0).astype(dtype)


def _unstack4(x, mask):
    L = RWKV_HEAD_DIM
    x = jnp.where(mask, x, 0.0)
    return sum(x[i * L:(i + 1) * L] for i in range(x.shape[0] // L))


def _rwkv_scan_body(m_ref, n_ref, ry_ref, y0_ref, rkk_ref, v_ref, g_ref, lng_ref, lnb_ref, o_ref, h_ref):
    S = RWKV_GROUP
    mask = _head_block_mask(S)
    head_mean = mask.astype(F32) * (1.0 / RWKV_HEAD_DIM)

    @pl.when(pl.program_id(0) == 0)
    def _():
        h_ref[...] = jnp.zeros_like(h_ref)

    slabs = [slice(i * S, (i + 1) * S) for i in range(RWKV_WIDTH // S)]
    each = lambda fn, *lists: [fn(*args) for args in zip(*lists)]
    state = [_bf(h_ref[i]) for i in range(len(slabs))]
    y = each(lambda sl, h: _dot(_bf(ry_ref[:, sl]), h) + y0_ref[:, sl], slabs, state)
    for i, sl in enumerate(slabs):
        h_ref[i] = _dot(_stack4(m_ref[:, sl], mask), state[i]) + _stack4(n_ref[:, sl], mask, F32)
    mean = each(lambda x: _dot(x, head_mean, HIGHEST), y)
    var = each(lambda x, mu: _dot(jnp.square(x - mu), head_mean, HIGHEST), y, mean)
    rk = each(lambda sl: _dot(rkk_ref[:, sl], mask.astype(F32), HIGHEST), slabs)
    for i, sl in enumerate(slabs):
        yn = (y[i] - mean[i]) * lax.rsqrt(var[i] + GN_EPS) * lng_ref[:, sl] + lnb_ref[:, sl]
        o_ref[:, sl] = ((yn + rk[i] * v_ref[:, sl]) * g_ref[:, sl]).astype(o_ref.dtype)


def _rwkv7(proj, lora_col0, mu_rkv, mu_lora, w0, w_up, a0, a_up, g_up, k_k, k_a, r_k, ln_g, ln_b, out_dtype=F32):
    t = proj.shape[0]
    nc = t // CHUNK
    W, L = RWKV_WIDTH, CHUNK
    lo = lora_col0 // LORA_PAD
    go = (lora_col0 + 2 * LORA_PAD) // GATE_LORA
    row_w = lambda i: (i, 0)
    vec = lambda x: x.reshape(1, -1)
    pad_rows = lambda x: jnp.pad(x, ((0, LORA_PAD - x.shape[0]), (0, 0)))
    full = lambda shape: pl.BlockSpec(shape, lambda i: (0,) * len(shape))
    rows_per_prev = L // PREV_ROWS
    cur = lambda width, blk: pl.BlockSpec((L, width), lambda i: (i, blk))
    prev = lambda width, blk: pl.BlockSpec((PREV_ROWS, width), lambda i: (jnp.maximum(i * rows_per_prev - 1, 0), blk))
    columns = [(W, 0), (W, 1), (W, 2), (LORA_PAD, lo), (LORA_PAD, lo + 1), (GATE_LORA, go)]
    mus = [mu_rkv[:W], mu_rkv[W:2 * W], mu_rkv[2 * W:], mu_lora[:LORA_PAD], mu_lora[LORA_PAD:2 * LORA_PAD],
           mu_lora[2 * LORA_PAD:]]
    m, n, ry, y0, rkk, v_s, g = pl.pallas_call(
        _rwkv_chunk_body,
        grid=(nc,),
        in_specs=(
            [cur(w, b) for w, b in columns] + [prev(w, b) for w, b in columns]
            + [full((1, w)) for w, _ in columns]
            + [full((1, W)), full((LORA_PAD, W)), full((1, W)), full((LORA_PAD, W)), full((GATE_LORA, W)),
               full((1, W)), full((1, W)), full((1, W))]),
        out_specs=[pl.BlockSpec((L, W), row_w)] * 7,
        out_shape=[jax.ShapeDtypeStruct((t, W), F32)] * 7,
        compiler_params=_params("arbitrary"),
    )(*([proj] * 12), *[vec(mu) for mu in mus],
      vec(w0), pad_rows(w_up), vec(a0), pad_rows(a_up), g_up, vec(k_k), vec(k_a), vec(r_k))

    return pl.pallas_call(
        _rwkv_scan_body,
        grid=(nc,),
        in_specs=[pl.BlockSpec((L, W), row_w)] * 7 + [full((1, W)), full((1, W))],
        out_specs=pl.BlockSpec((L, W), row_w),
        out_shape=jax.ShapeDtypeStruct((t, W), out_dtype),
        scratch_shapes=[pltpu.VMEM((W // RWKV_GROUP, RWKV_GROUP, RWKV_GROUP), F32)],
        compiler_params=_params("arbitrary"),
    )(m, n, ry, y0, rkk, v_s, g, vec(ln_g), vec(ln_b))


def _mlstm_body(q_ref, k_ref, qp_ref, kp_ref, cw_ref, v_ref, o_ref, gates_ref, bif_ref, ng_ref, out_ref,
                c_ref, n_ref, m_ref):
    L, D = CHUNK, MLSTM_HEAD_DIM
    first_chunk = pl.program_id(0) == 0

    @pl.when(first_chunk)
    def _():
        c_ref[...] = jnp.zeros_like(c_ref)
        n_ref[...] = jnp.zeros_like(n_ref)
        m_ref[...] = jnp.zeros_like(m_ref)

    def conv_silu(x_ref, prev_ref, w):
        x = x_ref[...]
        before = jnp.where(first_chunk, 0.0, prev_ref[...])
        head_rows = lax.broadcasted_iota(jnp.int32, before.shape, 0)
        acc = x * w[MLSTM_CONV - 1:MLSTM_CONV, :]
        for d in range(1, MLSTM_CONV):
            rolled = pltpu.roll(x, d, axis=0)
            top = jnp.where(head_rows < d, pltpu.roll(before, d, axis=0), rolled[:PREV_ROWS])
            shifted = jnp.concatenate([top, rolled[PREV_ROWS:]], axis=0)
            acc = acc + shifted * w[MLSTM_CONV - 1 - d:MLSTM_CONV - d, :]
        return acc * jax.nn.sigmoid(acc)

    q_all = conv_silu(q_ref, qp_ref, cw_ref[:, :MLSTM_WIDTH])
    k_all = conv_silu(k_ref, kp_ref, cw_ref[:, MLSTM_WIDTH:])

    row = lax.broadcasted_iota(jnp.int32, (L, L), 0)
    col = lax.broadcasted_iota(jnp.int32, (L, L), 1)
    incl = col <= row
    eye = (col == row).astype(F32)
    gates = gates_ref[...] + bif_ref[...]
    b_all = _dot(incl.astype(F32), jax.nn.log_sigmoid(gates), HIGHEST)
    to_row = lambda c: jnp.sum(c * eye, axis=0, keepdims=True)

    for h in range(MLSTM_HEADS):
        sl = slice(h * D, (h + 1) * D)
        q = q_all[:, sl] * (D ** -0.5)
        k = k_all[:, sl]
        v = v_ref[:, sl]
        i_col = gates[:, h:h + 1]
        b_col = b_all[:, MLSTM_HEADS + h:MLSTM_HEADS + h + 1]
        m_prev = m_ref[h]
        c_mat = c_ref[h]
        n_vec = n_ref[h]
        d_mat = jnp.where(incl, b_col - to_row(b_col) + to_row(i_col), -jnp.inf)
        inter = b_col + m_prev
        m_row = jnp.maximum(inter, jnp.max(d_mat, axis=-1, keepdims=True))
        w_inter = jnp.exp(inter - m_row)
        q_b, k_b = _bf(q), _bf(k)
        s = _dot_nt(q_b, k_b) * jnp.exp(d_mat - m_row)
        num = _dot(_bf(s), _bf(v)) + w_inter * _dot_nt(q_b, _bf(c_mat))
        den = jnp.sum(s, axis=-1, keepdims=True) + w_inter * jnp.sum(q * n_vec, axis=-1, keepdims=True)
        hh = num / jnp.maximum(jnp.abs(den), jnp.exp(-m_row))
        b_last = b_col[L - 1:L, :]
        d_state = b_last - b_col + i_col
        m_new = jnp.maximum(b_last + m_prev, jnp.max(d_state, axis=0, keepdims=True))
        w_state = jnp.exp(d_state - m_new)
        w_carry = jnp.exp(b_last + m_prev - m_new)
        c_ref[h] = w_carry * c_mat + _dot_tn(_bf(w_state * v), k_b)
        n_ref[h] = w_carry * n_vec + jnp.sum(w_state * k, axis=0, keepdims=True)
        m_ref[h] = m_new
        hn = hh * lax.rsqrt(jnp.mean(hh * hh, axis=-1, keepdims=True) + NORM_EPS) * ng_ref[:, sl]
        out_ref[:, sl] = (hn * jax.nn.sigmoid(o_ref[:, sl])).astype(out_ref.dtype)


def _mlstm(proj, q_blk, v_blk, o_blk, gates_blk, conv_w, b_if, norm_g, out_dtype=F32):
    t = proj.shape[0]
    L, W = CHUNK, MLSTM_WIDTH
    bif = jnp.pad(b_if, (0, LORA_PAD - b_if.shape[0])).reshape(1, LORA_PAD)
    prev = lambda blk: pl.BlockSpec((PREV_ROWS, W), lambda i: (jnp.maximum(i * (L // PREV_ROWS) - 1, 0), blk))
    return pl.pallas_call(
        _mlstm_body,
        grid=(t // L,),
        in_specs=[
            pl.BlockSpec((L, W), lambda i: (i, q_blk)),
            pl.BlockSpec((L, W), lambda i: (i, q_blk + 1)),
            prev(q_blk), prev(q_blk + 1),
            pl.BlockSpec((MLSTM_CONV, 2 * W), lambda i: (0, 0)),
            pl.BlockSpec((L, W), lambda i: (i, v_blk)),
            pl.BlockSpec((L, W), lambda i: (i, o_blk)),
            pl.BlockSpec((L, LORA_PAD), lambda i: (i, gates_blk)),
            pl.BlockSpec((1, LORA_PAD), lambda i: (0, 0)),
            pl.BlockSpec((1, W), lambda i: (0, 0)),
        ],
        out_specs=pl.BlockSpec((L, W), lambda i: (i, 0)),
        out_shape=jax.ShapeDtypeStruct((t, W), out_dtype),
        scratch_shapes=[pltpu.VMEM((MLSTM_HEADS, MLSTM_HEAD_DIM, MLSTM_HEAD_DIM), F32),
                        pltpu.VMEM((MLSTM_HEADS, 1, MLSTM_HEAD_DIM), F32),
                        pltpu.VMEM((MLSTM_HEADS, 1, 1), F32)],
        compiler_params=_params("arbitrary"),
    )(proj, proj, proj, proj, conv_w, proj, proj, proj, bif, norm_g.reshape(1, W))


def _fox_qkv_prep_body(x_ref, g_ref, o_ref):
    D = FOX_HEAD_DIM

    @pl.when(pl.program_id(1) < 2)
    def _():
        for h in range(FOX_HEADS):
            sl = slice(h * D, (h + 1) * D)
            x = x_ref[:, sl]
            y = x * lax.rsqrt(jnp.mean(x * x, axis=-1, keepdims=True) + NORM_EPS) * g_ref[...]
            o_ref[:, sl] = y.astype(o_ref.dtype)

    @pl.when(pl.program_id(1) == 2)
    def _():
        o_ref[...] = x_ref[...].astype(o_ref.dtype)


def _fox_qkv_prep(proj, qn_g, kn_g, tm=512):
    t = proj.shape[0]
    W = FOX_HEADS * FOX_HEAD_DIM
    gains = jnp.stack([qn_g * (FOX_HEAD_DIM ** -0.5 * LOG2E), kn_g, jnp.ones_like(kn_g)]).reshape(3, 1, FOX_HEAD_DIM)
    return pl.pallas_call(
        _fox_qkv_prep_body,
        grid=(t // tm, 3),
        in_specs=[pl.BlockSpec((tm, W), lambda i, j: (i, j)),
                  pl.BlockSpec((None, 1, FOX_HEAD_DIM), lambda i, j: (j, 0, 0))],
        out_specs=pl.BlockSpec((tm, W), lambda i, j: (i, j)),
        out_shape=jax.ShapeDtypeStruct((t, 3 * W), BF16),
        compiler_params=_params("arbitrary", "arbitrary"),
    )(proj, gains)


def _log_forget_cumsum_body(f_ref, b_ref, o_ref, carry_ref):
    tm = f_ref.shape[0]

    @pl.when(pl.program_id(0) == 0)
    def _():
        carry_ref[...] = jnp.zeros_like(carry_ref)

    row = lax.broadcasted_iota(jnp.int32, (tm, tm), 0)
    col = lax.broadcasted_iota(jnp.int32, (tm, tm), 1)
    log_f = jax.nn.log_sigmoid(f_ref[...] + b_ref[...])
    cum = _dot((col <= row).astype(F32), log_f, HIGHEST) + carry_ref[...]
    o_ref[...] = cum * LOG2E
    carry_ref[...] = cum[tm - 1:tm, :]


def _log_forget_cumsum(f_pre, b_f, tm=512):
    t, nh = f_pre.shape
    return pl.pallas_call(
        _log_forget_cumsum_body,
        grid=(t // tm,),
        in_specs=[pl.BlockSpec((tm, nh), lambda i: (i, 0)), pl.BlockSpec((1, nh), lambda i: (0, 0))],
        out_specs=pl.BlockSpec((tm, nh), lambda i: (i, 0)),
        out_shape=jax.ShapeDtypeStruct((t, nh), F32),
        scratch_shapes=[pltpu.VMEM((1, nh), F32)],
        compiler_params=_params("arbitrary"),
    )(f_pre, b_f.reshape(1, nh))


def _fox_body(q_ref, k_ref, v_ref, g_ref, cq_ref, ck_ref, o_ref, sa_ref, sb_ref, va_ref, *, tq, tk, ts):
    qi = pl.program_id(1)
    n_sub = tq // ts
    subs = range(n_sub)
    rows = [pl.ds(i * ts, ts) for i in subs]

    def fill(dst_ref, kb):
        start = pl.multiple_of(kb * tk, tk)
        for i in subs:
            dst_ref[rows[i], :] = _dot_nt(q_ref[rows[i], :], k_ref[pl.ds(start, tk), :])

    def absorb(src_ref, kb, carry, diagonal):
        start = pl.multiple_of(kb * tk, tk)
        out = []
        for i in subs:
            m_prev, acc = carry[i]
            width = (i + 1) * ts if diagonal else tk
            s = src_ref[rows[i], :width] + cq_ref[rows[i], :] - ck_ref[kb][:, :width]
            if diagonal:
                row = i * ts + lax.broadcasted_iota(jnp.int32, (ts, width), 0)
                col = lax.broadcasted_iota(jnp.int32, (ts, width), 1)
                s = jnp.where(col <= row, s, -jnp.inf)
            m_new = jnp.maximum(m_prev, jnp.max(s, axis=-1, keepdims=True))
            p = jnp.exp2(s - m_new)
            acc = jnp.exp2(m_prev - m_new) * acc + _dot(_bf(p), va_ref[pl.ds(start, width), :])
            out.append((m_new, acc))
        return tuple(out)

    def pair(pi, carry):
        fill(sb_ref, 2 * pi + 1)
        carry = absorb(sa_ref, 2 * pi, carry, False)
        fill(sa_ref, 2 * pi + 2)
        return absorb(sb_ref, 2 * pi + 1, carry, False)

    D = FOX_HEAD_DIM

    @pl.when(qi == 0)
    def _():
        va_ref[:, :D] = v_ref[...]
        va_ref[:, D:] = jnp.ones((va_ref.shape[0], D), va_ref.dtype)

    init = tuple((jnp.full((ts, 1), -jnp.inf, F32), jnp.zeros((ts, 2 * D), F32)) for _ in range(n_sub))
    fill(sa_ref, 0)
    carry = lax.fori_loop(0, qi // 2, pair, init)

    def odd_tail(carry):
        fill(sb_ref, qi)
        carry = absorb(sa_ref, qi - 1, carry, False)
        return absorb(sb_ref, qi, carry, True)

    final = lax.cond(qi % 2 == 1, odd_tail, lambda c: absorb(sa_ref, qi, c, True), carry)
    for i in range(n_sub):
        _, acc = final[i]
        o_ref[rows[i], :] = (acc[:, :D] / acc[:, D:] * jax.nn.sigmoid(g_ref[rows[i], :])).astype(o_ref.dtype)


def _fox_attention(qkv, proj, g_blk0, cum, tq=512):
    t = qkv.shape[0]
    D, H = FOX_HEAD_DIM, FOX_HEADS
    tk = tq
    cum_t = cum.T
    cq = cum_t.reshape(H, t, 1)
    ck = cum_t.reshape(H, t // tk, 1, tk)
    return pl.pallas_call(
        functools.partial(_fox_body, tq=tq, tk=tk, ts=FOX_SUB_TILE),
        grid=(H, t // tq),
        in_specs=[
            pl.BlockSpec((tq, D), lambda h, qi: (qi, h)),
            pl.BlockSpec((t, D), lambda h, qi: (0, H + h)),
            pl.BlockSpec((t, D), lambda h, qi: (0, 2 * H + h)),
            pl.BlockSpec((tq, D), lambda h, qi: (qi, g_blk0 + h)),
            pl.BlockSpec((None, tq, 1), lambda h, qi: (h, qi, 0)),
            pl.BlockSpec((None, t // tk, 1, tk), lambda h, qi: (h, 0, 0, 0)),
        ],
        out_specs=pl.BlockSpec((tq, D), lambda h, qi: (qi, h)),
        out_shape=jax.ShapeDtypeStruct((t, H * D), BF16),
        scratch_shapes=[pltpu.VMEM((tq, tk), F32), pltpu.VMEM((tq, tk), F32), pltpu.VMEM((t, 2 * D), BF16)],
        compiler_params=_params("arbitrary", "arbitrary"),
    )(qkv, qkv, qkv, proj, cq, ck)


def _xattn_body(q_ref, k_ref, v_ref, o_ref):
    s = _dot_nt(q_ref[...], k_ref[...]) * (XATTN_HEAD_DIM ** -0.5)
    p = jnp.exp(s - jnp.max(s, axis=-1, keepdims=True))
    o = _dot(p.astype(BF16), v_ref[...]) / jnp.sum(p, axis=-1, keepdims=True)
    o_ref[...] = o.astype(o_ref.dtype)


def _cross_attention(q, kv, tq=512):
    t = q.shape[0]
    n_mem = kv.shape[0]
    D, H = XATTN_HEAD_DIM, XATTN_HEADS
    return pl.pallas_call(
        _xattn_body,
        grid=(t // tq, H),
        in_specs=[pl.BlockSpec((tq, D), lambda i, h: (i, h)),
                  pl.BlockSpec((n_mem, D), lambda i, h: (0, h)),
                  pl.BlockSpec((n_mem, D), lambda i, h: (0, H + h))],
        out_specs=pl.BlockSpec((tq, D), lambda i, h: (i, h)),
        out_shape=jax.ShapeDtypeStruct((t, H * D), BF16),
        compiler_params=_params("arbitrary", "arbitrary"),
    )(q, kv, kv)


def _swiglu_up_body(x_ref, g_ref, wg_ref, wu_ref, o_ref, xn_ref):
    @pl.when(pl.program_id(1) == 0)
    def _():
        x = x_ref[...]
        xn_ref[...] = _bf(x * lax.rsqrt(jnp.mean(x * x, axis=-1, keepdims=True) + NORM_EPS) * g_ref[...])

    xn = xn_ref[...]
    a = _dot(xn, _bf(wg_ref[...]))
    b = _dot(xn, _bf(wu_ref[...]))
    o_ref[...] = (a * jax.nn.sigmoid(a) * b).astype(o_ref.dtype)


def _swiglu_up(x, norm_g, w_gate, w_up, tm=MM_ROW_TILE, tf=MM_COL_TILE):
    t, d = x.shape
    f = w_gate.shape[1]
    tm = min(tm, t)
    return pl.pallas_call(
        _swiglu_up_body,
        grid=(t // tm, f // tf),
        in_specs=[pl.BlockSpec((tm, d), lambda i, j: (i, 0)),
                  pl.BlockSpec((1, d), lambda i, j: (0, 0)),
                  pl.BlockSpec((d, tf), lambda i, j: (0, j)),
                  pl.BlockSpec((d, tf), lambda i, j: (0, j))],
        out_specs=pl.BlockSpec((tm, tf), lambda i, j: (i, j)),
        out_shape=jax.ShapeDtypeStruct((t, f), BF16),
        scratch_shapes=[pltpu.VMEM((tm, d), BF16)],
        compiler_params=_params("arbitrary", "arbitrary"),
    )(x, norm_g.reshape(1, d), w_gate, w_up)


def _mm_acc_body(x_ref, w_ref, res_ref, o_ref):
    @pl.when(pl.program_id(2) == 0)
    def _():
        o_ref[...] = res_ref[...]

    o_ref[...] += _dot(x_ref[...], _bf(w_ref[...]))


def _mm_acc(x, w, res, tm=MM_ROW_TILE, tn=MM_ROW_TILE, tk=MM_COL_TILE):
    t, k = x.shape
    n = w.shape[1]
    tm = min(tm, t)
    return pl.pallas_call(
        _mm_acc_body,
        grid=(t // tm, n // tn, k // tk),
        in_specs=[pl.BlockSpec((tm, tk), lambda i, j, kk: (i, kk)),
                  pl.BlockSpec((tk, tn), lambda i, j, kk: (kk, j)),
                  pl.BlockSpec((tm, tn), lambda i, j, kk: (i, j))],
        out_specs=pl.BlockSpec((tm, tn), lambda i, j, kk: (i, j)),
        out_shape=jax.ShapeDtypeStruct((t, n), F32),
        compiler_params=_params("arbitrary", "arbitrary", "arbitrary"),
    )(x, w, res)


def _rmsnorm_body(x_ref, g_ref, o_ref):
    x = x_ref[...]
    o_ref[...] = x * lax.rsqrt(jnp.mean(x * x, axis=-1, keepdims=True) + NORM_EPS) * g_ref[...]


def _rmsnorm(x, g, tm=512):
    t, d = x.shape
    return pl.pallas_call(
        _rmsnorm_body,
        grid=(t // tm,),
        in_specs=[pl.BlockSpec((tm, d), lambda i: (i, 0)), pl.BlockSpec((1, d), lambda i: (0, 0))],
        out_specs=pl.BlockSpec((tm, d), lambda i: (i, 0)),
        out_shape=jax.ShapeDtypeStruct((t, d), F32),
        compiler_params=_params("arbitrary"),
    )(x, g.reshape(1, d))


def _pad_cols(w, n=LORA_PAD):
    return jnp.pad(w, [(0, 0)] * (w.ndim - 1) + [(0, n - w.shape[-1])])


def _pack_ab_columns(w):
    a_cols = 3 * RWKV_WIDTH + DECAY_LORA + AAA_LORA + GATE_LORA
    a, b = w[..., :a_cols], w[..., a_cols:]
    o1 = 3 * RWKV_WIDTH
    o2 = o1 + DECAY_LORA
    o3 = o2 + AAA_LORA
    return jnp.concatenate([
        a[..., :o1], b[..., :4 * MLSTM_WIDTH],
        _pad_cols(a[..., o1:o2]), _pad_cols(a[..., o2:o3]), a[..., o3:],
        _pad_cols(b[..., 4 * MLSTM_WIDTH:])], axis=-1)


def _rwkv_mlstm_mixer(x, norm_g, w_in, mu, w0, w_up, a0, a_up, g_up, k_k, k_a, r_k, ln_g, ln_b,
                      conv_qk, b_if, mh_g, w_out):
    proj = _norm_mm(x, _pack_ab_columns(w_in), norm_g=norm_g)
    mu_p = _pack_ab_columns(jnp.concatenate([mu, jnp.zeros((4 * MLSTM_WIDTH + 2 * MLSTM_HEADS,), F32)]))
    o1 = 3 * RWKV_WIDTH
    o_lora = o1 + 4 * MLSTM_WIDTH
    lora_w = 2 * LORA_PAD + GATE_LORA
    y_a = _rwkv7(proj, o_lora, mu_p[:o1], mu_p[o_lora:o_lora + lora_w], w0, w_up, a0, a_up, g_up, k_k, k_a, r_k,
                 ln_g, ln_b, out_dtype=BF16)
    y_b = _mlstm(proj, o1 // MLSTM_WIDTH, o1 // MLSTM_WIDTH + 2, o1 // MLSTM_WIDTH + 3,
                 (o_lora + lora_w) // LORA_PAD, conv_qk, b_if, mh_g, out_dtype=BF16)

    x = _norm_mm(y_a, w_out[:RWKV_WIDTH], res=x)
    return _norm_mm(y_b, w_out[RWKV_WIDTH:], res=x)


def _fox_layer(x, norm_g, w_in, b_f, qn_g, kn_g, w_out):
    W = FOX_HEADS * FOX_HEAD_DIM
    proj = _norm_mm(x, w_in, norm_g=norm_g)
    qkv = _fox_qkv_prep(proj, qn_g, kn_g)
    cum = _log_forget_cumsum(proj[:, 4 * W:], b_f)
    o = _fox_attention(qkv, proj, 3 * FOX_HEADS, cum)
    return _norm_mm(o, w_out, res=x)


def _xattn_layer(x, norm_g, kv, wq, wo):
    q = _norm_mm(x, wq, norm_g=norm_g, out_dtype=BF16)
    return _norm_mm(_cross_attention(q, kv), wo, res=x)


def _moe_experts_body(tile_e_ref, n_valid_ref, tile_rows_ref, x_ref, g_ref, wg_ref, wu_ref, wd_ref, gate_ref,
                      o_ref, xn_ref):
    i = pl.program_id(0)
    j = pl.program_id(1)

    @pl.when(j == 0)
    def _():
        o_ref[...] = jnp.zeros_like(o_ref)
        x = x_ref[...]
        xn_ref[...] = _bf(x * lax.rsqrt(jnp.mean(x * x, axis=-1, keepdims=True) + NORM_EPS) * g_ref[...])

    wg, wu, wd = _bf(wg_ref[...]), _bf(wu_ref[...]), _bf(wd_ref[...])
    for sub in range(x_ref.shape[0] // MOE_SUB_ROWS):
        rows = pl.ds(sub * MOE_SUB_ROWS, MOE_SUB_ROWS)

        @pl.when(sub * MOE_SUB_ROWS < tile_rows_ref[i])
        def _():
            x = xn_ref[rows, :]
            a = _dot(x, wg)
            b = _dot(x, wu)
            h = a * jax.nn.sigmoid(a) * b
            o_ref[rows, :] += _dot(_bf(h), wd)

    @pl.when(j == pl.num_programs(1) - 1)
    def _():
        o_ref[...] = o_ref[...] * gate_ref[...]


def _moe_experts(xg, norm_g, gate_at, tile_e, n_valid, tile_rows, w_gate, w_up, w_down, tm, tf=256):
    p, d = xg.shape
    f = w_gate.shape[2]
    nf = f // tf

    def expert(i, te, nv):
        return te[jnp.minimum(i, nv[0] - 1)]

    def fblock(i, j, nv):
        return jnp.where(i < nv[0], j, nf - 1)

    grid_spec = pltpu.PrefetchScalarGridSpec(
        num_scalar_prefetch=3,
        grid=(p // tm, nf),
        in_specs=[
            pl.BlockSpec((tm, d), lambda i, j, te, nv, tr: (i, 0), pipeline_mode=pl.Buffered(1)),
            pl.BlockSpec((1, d), lambda i, j, te, nv, tr: (0, 0)),
            pl.BlockSpec((None, d, tf), lambda i, j, te, nv, tr: (expert(i, te, nv), 0, fblock(i, j, nv))),
            pl.BlockSpec((None, d, tf), lambda i, j, te, nv, tr: (expert(i, te, nv), 0, fblock(i, j, nv))),
            pl.BlockSpec((None, tf, d), lambda i, j, te, nv, tr: (expert(i, te, nv), fblock(i, j, nv), 0)),
            pl.BlockSpec((tm, 1), lambda i, j, te, nv, tr: (i, 0)),
        ],
        out_specs=pl.BlockSpec((tm, d), lambda i, j, te, nv, tr: (i, 0)),
        scratch_shapes=[pltpu.VMEM((tm, d), BF16)],
    )
    return pl.pallas_call(
        _moe_experts_body,
        grid_spec=grid_spec,
        out_shape=jax.ShapeDtypeStruct((p, d), F32),
        compiler_params=_params("arbitrary", "arbitrary"),
    )(tile_e, n_valid, tile_rows, xg, norm_g.reshape(1, d), w_gate, w_up, w_down, gate_at)


MOE_TILE = 1024
MOE_SUB_ROWS = 256


def _moe_layer(x, norm_g, w_router, w_gate, w_up, w_down):
    t = x.shape[0]
    tm = MOE_TILE
    logits = _norm_mm(x, _pad_cols(w_router), norm_g=norm_g, tm=512, mxu_dtype=F32, precision=HIGHEST)
    top_val, top_idx = lax.top_k(logits[:, :N_EXPERTS], TOP_K)
    gates = jax.nn.softmax(top_val, axis=-1)
    flat_e = top_idx.reshape(-1)
    onehot = (flat_e[:, None] == jnp.arange(N_EXPERTS)[None, :]).astype(jnp.int32)
    ranks = jnp.cumsum(onehot, axis=0) - onehot
    counts = jnp.sum(onehot, axis=0)
    padded = (counts + tm - 1) // tm * tm
    pad_end = jnp.cumsum(padded)
    dest = (pad_end - padded)[flat_e] + jnp.sum(ranks * onehot, axis=1)
    n_rows = TOP_K * t + N_EXPERTS * tm
    tok_at = (jnp.arange(n_rows, dtype=jnp.int32) % t).at[dest].set(jnp.arange(TOP_K * t, dtype=jnp.int32) // TOP_K)
    gate_at = jnp.zeros((n_rows,), F32).at[dest].set(gates.reshape(-1))
    tile_start = jnp.arange(n_rows // tm, dtype=jnp.int32) * tm
    tile_e = jnp.minimum(jnp.sum((tile_start[:, None] >= pad_end[None, :]).astype(jnp.int32), axis=1), N_EXPERTS - 1)
    n_valid = (pad_end[-1:] // tm).astype(jnp.int32)
    routed_end = pad_end - padded + counts
    tile_rows = jnp.clip(routed_end[tile_e] - tile_start, 0, tm).astype(jnp.int32)
    xg = jnp.take(x, tok_at, axis=0, mode="clip")
    y = _moe_experts(xg, norm_g, gate_at.reshape(n_rows, 1), tile_e, n_valid, tile_rows, w_gate, w_up, w_down, tm)
    dest2 = dest.reshape(t, TOP_K)
    return x + jnp.take(y, dest2[:, 0], axis=0, mode="clip") + jnp.take(y, dest2[:, 1], axis=0, mode="clip")


def kernel(x, mem, mem_norm_g, final_norm_g, norm_mix_g, norm_xattn_g, norm_ffn_g,
           xa_wq, xa_wkv, xa_wo,
           ab_w_in, ab_mu, rw_w0, rw_w_up, rw_a0, rw_a_up, rw_g_up, rw_k_k, rw_k_a, rw_r_k,
           rw_ln_g, rw_ln_b, ml_conv, ml_b_if, ml_norm_g, ab_w_out,
           ffn_w_gate, ffn_w_up, ffn_w_down,
           fox_w_in, fox_b_f, fox_qn_g, fox_kn_g, fox_w_out,
           moe_router, moe_w_gate, moe_w_up, moe_w_down):
    bsz, seq, d = x.shape
    assert bsz == 1 and d == D_MODEL and seq % 512 == 0
    xs = x[0]
    mem_s = mem[0]
    depth = norm_mix_g.shape[0]
    for layer in range(depth):
        j = layer // 2
        if layer % 2 == 0:
            xs = _rwkv_mlstm_mixer(xs, norm_mix_g[layer], ab_w_in[j], ab_mu[j], rw_w0[j], rw_w_up[j], rw_a0[j],
                                   rw_a_up[j], rw_g_up[j], rw_k_k[j], rw_k_a[j], rw_r_k[j], rw_ln_g[j],
                                   rw_ln_b[j], ml_conv[j], ml_b_if[j], ml_norm_g[j], ab_w_out[j])
        else:
            xs = _fox_layer(xs, norm_mix_g[layer], fox_w_in[j], fox_b_f[j], fox_qn_g[j], fox_kn_g[j],
                            fox_w_out[j])
        kv = _norm_mm(mem_s, xa_wkv[layer], norm_g=mem_norm_g, tm=mem_s.shape[0], out_dtype=BF16)
        xs = _xattn_layer(xs, norm_xattn_g[layer], kv, xa_wq[layer], xa_wo[layer])
        if layer % 2 == 0:
            hidden = _swiglu_up(xs, norm_ffn_g[layer], ffn_w_gate[j], ffn_w_up[j])
            xs = _mm_acc(hidden, ffn_w_down[j], xs)
        else:
            xs = _moe_layer(xs, norm_ffn_g[layer], moe_router[j], moe_w_gate[j], moe_w_up[j], moe_w_down[j])
    return _rmsnorm(xs, final_norm_g)[None]
```

```python
import functools

import jax
import jax.numpy as jnp
from jax import lax
from jax.experimental import pallas as pl
from jax.experimental.pallas import tpu as pltpu

F32 = jnp.float32
BF16 = jnp.bfloat16
HIGHEST = lax.Precision.HIGHEST

D_MODEL = 2048
CHUNK = 64
NORM_EPS = 1e-6
GN_EPS = 64e-5
RWKV_WIDTH = 1024
RWKV_HEAD_DIM = 64
RWKV_HEADS = 16
RWKV_GROUP = 256
DECAY_LORA = 96
AAA_LORA = 96
GATE_LORA = 256
PREV_ROWS = 8
LORA_PAD = 128
MLSTM_WIDTH = 1024
MLSTM_HEAD_DIM = 256
MLSTM_HEADS = 4
MLSTM_CONV = 4
FOX_HEAD_DIM = 128
FOX_HEADS = 16
XATTN_HEADS = 4
XATTN_HEAD_DIM = 512
D_FF = 5632
N_EXPERTS = 8
TOP_K = 2

VMEM_LIMIT_BYTES = 56 * 1024 * 1024
LOG2E = 1.4426950408889634
FOX_SUB_TILE = 128
MM_ROW_TILE = 1024
MM_COL_TILE = 512


def _params(*semantics):
    return pltpu.CompilerParams(dimension_semantics=semantics, vmem_limit_bytes=VMEM_LIMIT_BYTES)


def _bf(a):
    return a.astype(BF16)


def _dot(a, b, precision=None):
    return jnp.dot(a, b, preferred_element_type=F32, precision=precision)


def _dot_nt(a, b, precision=None):
    return lax.dot_general(a, b, (((1,), (1,)), ((), ())), preferred_element_type=F32, precision=precision)


def _dot_tn(a, b, precision=None):
    return lax.dot_general(a, b, (((0,), (0,)), ((), ())), preferred_element_type=F32, precision=precision)


def _norm_mm_body(*refs, has_norm, has_res, precision):
    it = iter(refs)
    x_ref = next(it)
    g_ref = next(it) if has_norm else None
    w_ref = next(it)
    res_ref = next(it) if has_res else None
    o_ref = next(it)
    xn_ref = next(it)

    @pl.when(pl.program_id(1) == 0)
    def _():
        x = x_ref[...].astype(F32)
        if has_norm:
            x = x * lax.rsqrt(jnp.mean(x * x, axis=-1, keepdims=True) + NORM_EPS) * g_ref[...]
        xn_ref[...] = x.astype(xn_ref.dtype)

    acc = _dot(xn_ref[...], w_ref[...].astype(xn_ref.dtype), precision)
    if has_res:
        acc = acc + res_ref[...]
    o_ref[...] = acc.astype(o_ref.dtype)


def _norm_mm(x, w, *, norm_g=None, res=None, tm=MM_ROW_TILE, tn=MM_COL_TILE, out_dtype=F32, mxu_dtype=BF16,
             precision=None):
    m, k = x.shape
    n = w.shape[1]
    tm, tn = min(tm, m), min(tn, n)
    assert m % tm == 0
    in_specs = [pl.BlockSpec((tm, k), lambda i, j: (i, 0))]
    args = [x]
    if norm_g is not None:
        in_specs.append(pl.BlockSpec((1, k), lambda i, j: (0, 0)))
        args.append(norm_g.reshape(1, k))
    in_specs.append(pl.BlockSpec((k, tn), lambda i, j: (0, j)))
    args.append(w)
    if res is not None:
        in_specs.append(pl.BlockSpec((tm, tn), lambda i, j: (i, j)))
        args.append(res)
    return pl.pallas_call(
        functools.partial(_norm_mm_body, has_norm=norm_g is not None, has_res=res is not None, precision=precision),
        grid=(m // tm, pl.cdiv(n, tn)),
        in_specs=in_specs,
        out_specs=pl.BlockSpec((tm, tn), lambda i, j: (i, j)),
        out_shape=jax.ShapeDtypeStruct((m, n), out_dtype),
        scratch_shapes=[pltpu.VMEM((tm, k), mxu_dtype)],
        compiler_params=_params("arbitrary", "arbitrary"),
    )(*args)


def _rwkv_chunk_body(r_ref, k_ref, v_ref, wlo_ref, alo_ref, glo_ref,
                     rp_ref, kp_ref, vp_ref, wlop_ref, alop_ref, glop_ref,
                     mur_ref, muk_ref, muv_ref, muw_ref, mua_ref, mug_ref,
                     w0_ref, wup_ref, a0_ref, aup_ref, gup_ref, kk_ref, ka_ref, rk_ref,
                     m_ref, n_ref, ry_ref, y0_ref, rkk_ref, vs_ref, g_ref):
    L, N = CHUNK, RWKV_HEAD_DIM
    first_chunk = pl.program_id(0) == 0

    def token_shift(x_ref, prev_ref, mu_ref):
        x = x_ref[...]
        before = jnp.where(first_chunk, 0.0, prev_ref[PREV_ROWS - 1:PREV_ROWS, :])
        first_row = lax.broadcasted_iota(jnp.int32, x.shape, 0) == 0
        prev = jnp.where(first_row, before, pltpu.roll(x, 1, axis=0))
        return x + (prev - x) * mu_ref[...]

    r = token_shift(r_ref, rp_ref, mur_ref)
    k = token_shift(k_ref, kp_ref, muk_ref)
    v = token_shift(v_ref, vp_ref, muv_ref)
    w_lo = token_shift(wlo_ref, wlop_ref, muw_ref)
    a_lo = token_shift(alo_ref, alop_ref, mua_ref)
    g_lo = token_shift(glo_ref, glop_ref, mug_ref)
    log_w = -jax.nn.softplus(-(w0_ref[...] + _dot(jnp.tanh(w_lo), wup_ref[...], HIGHEST))) - 0.5
    lw = -jnp.exp(log_w)
    a = jax.nn.sigmoid(a0_ref[...] + _dot(a_lo, aup_ref[...], HIGHEST))
    g_ref[...] = _dot(jax.nn.sigmoid(g_lo), gup_ref[...], HIGHEST)
    k_mod = k * (1.0 + (a - 1.0) * ka_ref[...])
    rkk_ref[...] = r * k_mod * rk_ref[...]
    vs_ref[...] = v
    kk_raw = k * kk_ref[...]

    row = lax.broadcasted_iota(jnp.int32, (L, L), 0)
    col = lax.broadcasted_iota(jnp.int32, (L, L), 1)
    cum = _dot((col <= row).astype(F32), lw, HIGHEST)
    cum_last = cum[L - 1:L, :]
    g_in = jnp.exp(cum)
    g_prev = jnp.exp(cum - lw)
    g_inv = jnp.exp(-cum)
    g_tail = jnp.exp(cum_last - cum)
    g_last = jnp.exp(cum_last)

    S = RWKV_GROUP
    stack_mask = _head_block_mask(S)
    t_idx = lax.broadcasted_iota(jnp.int32, (L, S), 0)
    s_idx = lax.broadcasted_iota(jnp.int32, (L, S), 1) % N
    incl = s_idx <= t_idx
    strict = s_idx < t_idx
    eye = (s_idx == t_idx).astype(F32)
    blk16 = (t_idx // 16) == (s_idx // 16)
    blk32 = (t_idx // 32) == (s_idx // 32)
    stack = functools.partial(_stack4, mask=stack_mask)
    unstack = functools.partial(_unstack4, mask=stack_mask)

    slabs = [slice(i * S, (i + 1) * S) for i in range(RWKV_WIDTH // S)]
    each = lambda fn, *lists: [fn(*args) for args in zip(*lists)]
    ones_bd = stack_mask.astype(F32)

    kk_g = each(lambda sl: kk_raw[:, sl], slabs)
    ssq = each(lambda x: _dot(x * x, ones_bd, HIGHEST), kk_g)
    kk_g = each(lambda x, s2: x / jnp.maximum(jnp.sqrt(s2), 1e-12), kk_g, ssq)
    b_g = each(lambda x, sl: x * a[:, sl], kk_g, slabs)
    rg = each(lambda sl: r[:, sl] * g_in[:, sl], slabs)
    kkg = each(lambda x, sl: x * g_prev[:, sl], kk_g, slabs)
    v_s = each(lambda sl: stack(v[:, sl]), slabs)
    kd_s = each(lambda sl: stack(k_mod[:, sl] * g_inv[:, sl]), slabs)
    bd_s = each(lambda x, sl: stack(x * g_inv[:, sl]), b_g, slabs)
    kdg = each(lambda sl: _bf(k_mod[:, sl] * g_tail[:, sl]), slabs)
    bdg = each(lambda x, sl: _bf(x * g_tail[:, sl]), b_g, slabs)

    lhs = each(lambda x, y: _bf(jnp.concatenate([x, y], axis=0)), kkg, rg)
    with_k = each(_dot_nt, lhs, kd_s)
    with_b = each(_dot_nt, lhs, bd_s)
    a_k = each(lambda x: jnp.where(strict, x[:L], 0.0), with_k)
    a_b = each(lambda x: jnp.where(strict, x[:L], 0.0), with_b)
    ar_k = each(lambda x: _bf(jnp.where(incl, x[L:], 0.0)), with_k)
    ar_b = each(lambda x: _bf(jnp.where(incl, x[L:], 0.0)), with_b)

    pw = each(lambda x: jnp.where(blk16, -x, 0.0), a_b)
    t_inv = each(lambda x: eye + x, pw)
    pw_s = each(stack, pw)
    for _ in range(3):
        pw = each(lambda x, xs: _dot(_bf(x), xs), pw, pw_s)
        pw_s = each(stack, pw)
        t_inv = each(lambda t, xs: t + _dot(_bf(t), xs), t_inv, pw_s)
    for off_mask in (blk32 & (~blk16), ~blk32):
        off_s = each(lambda x: stack(jnp.where(off_mask, x, 0.0)), a_b)
        left = each(lambda t, o: _bf(_dot(_bf(t), o)), t_inv, off_s)
        t_inv = each(lambda t, lt: t - _dot(lt, stack(t)), t_inv, left)
    t_b = each(_bf, t_inv)

    p = each(lambda t, x: _dot(t, stack(x)), t_b, kkg)
    akv = each(lambda x, vs: _dot(_bf(x), vs), a_k, v_s)
    q = each(lambda t, x: _dot(t, stack(x)), t_b, akv)
    for i, sl in enumerate(slabs):
        ry_ref[:, sl] = rg[i] - _dot(ar_b[i], stack(p[i]))
        y0_ref[:, sl] = _dot(ar_k[i], v_s[i]) - _dot(ar_b[i], stack(q[i]))
        m_ref[:, sl] = eye * g_last[:, sl] - unstack(_dot_tn(bdg[i], _bf(p[i])))
        n_ref[:, sl] = unstack(_dot_tn(kdg[i], _bf(v[:, sl])) - _dot_tn(bdg[i], _bf(q[i])))


def _head_block_mask(s):
    row = lax.broadcasted_iota(jnp.int32, (s, s), 0)
    col = lax.broadcasted_iota(jnp.int32, (s, s), 1)
    return (row // RWKV_HEAD_DIM) == (col // RWKV_HEAD_DIM)


def _stack4(x, mask, dtype=BF16):
    reps = mask.shape[0] // x.shape[0]
    return jnp.where(mask, jnp.concatenate([x] * reps, axis=0), 0.0).astype(dtype)


def _unstack4(x, mask):
    L = RWKV_HEAD_DIM
    x = jnp.where(mask, x, 0.0)
    return sum(x[i * L:(i + 1) * L] for i in range(x.shape[0] // L))


def _rwkv_scan_body(m_ref, n_ref, ry_ref, y0_ref, rkk_ref, v_ref, g_ref, lng_ref, lnb_ref, o_ref, h_ref):
    S = RWKV_GROUP
    mask = _head_block_mask(S)
    head_mean = mask.astype(F32) * (1.0 / RWKV_HEAD_DIM)

    @pl.when(pl.program_id(0) == 0)
    def _():
        h_ref[...] = jnp.zeros_like(h_ref)

    slabs = [slice(i * S, (i + 1) * S) for i in range(RWKV_WIDTH // S)]
    each = lambda fn, *lists: [fn(*args) for args in zip(*lists)]
    state = [_bf(h_ref[i]) for i in range(len(slabs))]
    y = each(lambda sl, h: _dot(_bf(ry_ref[:, sl]), h) + y0_ref[:, sl], slabs, state)
    for i, sl in enumerate(slabs):
        h_ref[i] = _dot(_stack4(m_ref[:, sl], mask), state[i]) + _stack4(n_ref[:, sl], mask, F32)
    mean = each(lambda x: _dot(x, head_mean, HIGHEST), y)
    var = each(lambda x, mu: _dot(jnp.square(x - mu), head_mean, HIGHEST), y, mean)
    rk = each(lambda sl: _dot(rkk_ref[:, sl], mask.astype(F32), HIGHEST), slabs)
    for i, sl in enumerate(slabs):
        yn = (y[i] - mean[i]) * lax.rsqrt(var[i] + GN_EPS) * lng_ref[:, sl] + lnb_ref[:, sl]
        o_ref[:, sl] = ((yn + rk[i] * v_ref[:, sl]) * g_ref[:, sl]).astype(o_ref.dtype)


def _rwkv7(proj, lora_col0, mu_rkv, mu_lora, w0, w_up, a0, a_up, g_up, k_k, k_a, r_k, ln_g, ln_b, out_dtype=F32):
    t = proj.shape[0]
    nc = t // CHUNK
    W, L = RWKV_WIDTH, CHUNK
    lo = lora_col0 // LORA_PAD
    go = (lora_col0 + 2 * LORA_PAD) // GATE_LORA
    row_w = lambda i: (i, 0)
    vec = lambda x: x.reshape(1, -1)
    pad_rows = lambda x: jnp.pad(x, ((0, LORA_PAD - x.shape[0]), (0, 0)))
    full = lambda shape: pl.BlockSpec(shape, lambda i: (0,) * len(shape))
    rows_per_prev = L // PREV_ROWS
    cur = lambda width, blk: pl.BlockSpec((L, width), lambda i: (i, blk))
    prev = lambda width, blk: pl.BlockSpec((PREV_ROWS, width), lambda i: (jnp.maximum(i * rows_per_prev - 1, 0), blk))
    columns = [(W, 0), (W, 1), (W, 2), (LORA_PAD, lo), (LORA_PAD, lo + 1), (GATE_LORA, go)]
    mus = [mu_rkv[:W], mu_rkv[W:2 * W], mu_rkv[2 * W:], mu_lora[:LORA_PAD], mu_lora[LORA_PAD:2 * LORA_PAD],
           mu_lora[2 * LORA_PAD:]]
    m, n, ry, y0, rkk, v_s, g = pl.pallas_call(
        _rwkv_chunk_body,
        grid=(nc,),
        in_specs=(
            [cur(w, b) for w, b in columns] + [prev(w, b) for w, b in columns]
            + [full((1, w)) for w, _ in columns]
            + [full((1, W)), full((LORA_PAD, W)), full((1, W)), full((LORA_PAD, W)), full((GATE_LORA, W)),
               full((1, W)), full((1, W)), full((1, W))]),
        out_specs=[pl.BlockSpec((L, W), row_w)] * 7,
        out_shape=[jax.ShapeDtypeStruct((t, W), F32)] * 7,
        compiler_params=_params("arbitrary"),
    )(*([proj] * 12), *[vec(mu) for mu in mus],
      vec(w0), pad_rows(w_up), vec(a0), pad_rows(a_up), g_up, vec(k_k), vec(k_a), vec(r_k))

    return pl.pallas_call(
        _rwkv_scan_body,
        grid=(nc,),
        in_specs=[pl.BlockSpec((L, W), row_w)] * 7 + [full((1, W)), full((1, W))],
        out_specs=pl.BlockSpec((L, W), row_w),
        out_shape=jax.ShapeDtypeStruct((t, W), out_dtype),
        scratch_shapes=[pltpu.VMEM((W // RWKV_GROUP, RWKV_GROUP, RWKV_GROUP), F32)],
        compiler_params=_params("arbitrary"),
    )(m, n, ry, y0, rkk, v_s, g, vec(ln_g), vec(ln_b))


def _mlstm_body(q_ref, k_ref, qp_ref, kp_ref, cw_ref, v_ref, o_ref, gates_ref, bif_ref, ng_ref, out_ref,
                c_ref, n_ref, m_ref):
    L, D = CHUNK, MLSTM_HEAD_DIM
    first_chunk = pl.program_id(0) == 0

    @pl.when(first_chunk)
    def _():
        c_ref[...] = jnp.zeros_like(c_ref)
        n_ref[...] = jnp.zeros_like(n_ref)
        m_ref[...] = jnp.zeros_like(m_ref)

    def conv_silu(x_ref, prev_ref, w):
        x = x_ref[...]
        before = jnp.where(first_chunk, 0.0, prev_ref[...])
        head_rows = lax.broadcasted_iota(jnp.int32, before.shape, 0)
        acc = x * w[MLSTM_CONV - 1:MLSTM_CONV, :]
        for d in range(1, MLSTM_CONV):
            rolled = pltpu.roll(x, d, axis=0)
            top = jnp.where(head_rows < d, pltpu.roll(before, d, axis=0), rolled[:PREV_ROWS])
            shifted = jnp.concatenate([top, rolled[PREV_ROWS:]], axis=0)
            acc = acc + shifted * w[MLSTM_CONV - 1 - d:MLSTM_CONV - d, :]
        return acc * jax.nn.sigmoid(acc)

    q_all = conv_silu(q_ref, qp_ref, cw_ref[:, :MLSTM_WIDTH])
    k_all = conv_silu(k_ref, kp_ref, cw_ref[:, MLSTM_WIDTH:])

    row = lax.broadcasted_iota(jnp.int32, (L, L), 0)
    col = lax.broadcasted_iota(jnp.int32, (L, L), 1)
    incl = col <= row
    eye = (col == row).astype(F32)
    gates = gates_ref[...] + bif_ref[...]
    b_all = _dot(incl.astype(F32), jax.nn.log_sigmoid(gates), HIGHEST)
    to_row = lambda c: jnp.sum(c * eye, axis=0, keepdims=True)

    for h in range(MLSTM_HEADS):
        sl = slice(h * D, (h + 1) * D)
        q = q_all[:, sl] * (D ** -0.5)
        k = k_all[:, sl]
        v = v_ref[:, sl]
        i_col = gates[:, h:h + 1]
        b_col = b_all[:, MLSTM_HEADS + h:MLSTM_HEADS + h + 1]
        m_prev = m_ref[h]
        c_mat = c_ref[h]
        n_vec = n_ref[h]
        d_mat = jnp.where(incl, b_col - to_row(b_col) + to_row(i_col), -jnp.inf)
        inter = b_col + m_prev
        m_row = jnp.maximum(inter, jnp.max(d_mat, axis=-1, keepdims=True))
        w_inter = jnp.exp(inter - m_row)
        q_b, k_b = _bf(q), _bf(k)
        s = _dot_nt(q_b, k_b) * jnp.exp(d_mat - m_row)
        num = _dot(_bf(s), _bf(v)) + w_inter * _dot_nt(q_b, _bf(c_mat))
        den = jnp.sum(s, axis=-1, keepdims=True) + w_inter * jnp.sum(q * n_vec, axis=-1, keepdims=True)
        hh = num / jnp.maximum(jnp.abs(den), jnp.exp(-m_row))
        b_last = b_col[L - 1:L, :]
        d_state = b_last - b_col + i_col
        m_new = jnp.maximum(b_last + m_prev, jnp.max(d_state, axis=0, keepdims=True))
        w_state = jnp.exp(d_state - m_new)
        w_carry = jnp.exp(b_last + m_prev - m_new)
        c_ref[h] = w_carry * c_mat + _dot_tn(_bf(w_state * v), k_b)
        n_ref[h] = w_carry * n_vec + jnp.sum(w_state * k, axis=0, keepdims=True)
        m_ref[h] = m_new
        hn = hh * lax.rsqrt(jnp.mean(hh * hh, axis=-1, keepdims=True) + NORM_EPS) * ng_ref[:, sl]
        out_ref[:, sl] = (hn * jax.nn.sigmoid(o_ref[:, sl])).astype(out_ref.dtype)


def _mlstm(proj, q_blk, v_blk, o_blk, gates_blk, conv_w, b_if, norm_g, out_dtype=F32):
    t = proj.shape[0]
    L, W = CHUNK, MLSTM_WIDTH
    bif = jnp.pad(b_if, (0, LORA_PAD - b_if.shape[0])).reshape(1, LORA_PAD)
    prev = lambda blk: pl.BlockSpec((PREV_ROWS, W), lambda i: (jnp.maximum(i * (L // PREV_ROWS) - 1, 0), blk))
    return pl.pallas_call(
        _mlstm_body,
        grid=(t // L,),
        in_specs=[
            pl.BlockSpec((L, W), lambda i: (i, q_blk)),
            pl.BlockSpec((L, W), lambda i: (i, q_blk + 1)),
            prev(q_blk), prev(q_blk + 1),
            pl.BlockSpec((MLSTM_CONV, 2 * W), lambda i: (0, 0)),
            pl.BlockSpec((L, W), lambda i: (i, v_blk)),
            pl.BlockSpec((L, W), lambda i: (i, o_blk)),
            pl.BlockSpec((L, LORA_PAD), lambda i: (i, gates_blk)),
            pl.BlockSpec((1, LORA_PAD), lambda i: (0, 0)),
            pl.BlockSpec((1, W), lambda i: (0, 0)),
        ],
        out_specs=pl.BlockSpec((L, W), lambda i: (i, 0)),
        out_shape=jax.ShapeDtypeStruct((t, W), out_dtype),
        scratch_shapes=[pltpu.VMEM((MLSTM_HEADS, MLSTM_HEAD_DIM, MLSTM_HEAD_DIM), F32),
                        pltpu.VMEM((MLSTM_HEADS, 1, MLSTM_HEAD_DIM), F32),
                        pltpu.VMEM((MLSTM_HEADS, 1, 1), F32)],
        compiler_params=_params("arbitrary"),
    )(proj, proj, proj, proj, conv_w, proj, proj, proj, bif, norm_g.reshape(1, W))


def _fox_qkv_prep_body(x_ref, g_ref, o_ref):
    D = FOX_HEAD_DIM

    @pl.when(pl.program_id(1) < 2)
    def _():
        for h in range(FOX_HEADS):
            sl = slice(h * D, (h + 1) * D)
            x = x_ref[:, sl]
            y = x * lax.rsqrt(jnp.mean(x * x, axis=-1, keepdims=True) + NORM_EPS) * g_ref[...]
            o_ref[:, sl] = y.astype(o_ref.dtype)

    @pl.when(pl.program_id(1) == 2)
    def _():
        o_ref[...] = x_ref[...].astype(o_ref.dtype)


def _fox_qkv_prep(proj, qn_g, kn_g, tm=512):
    t = proj.shape[0]
    W = FOX_HEADS * FOX_HEAD_DIM
    gains = jnp.stack([qn_g * (FOX_HEAD_DIM ** -0.5 * LOG2E), kn_g, jnp.ones_like(kn_g)]).reshape(3, 1, FOX_HEAD_DIM)
    return pl.pallas_call(
        _fox_qkv_prep_body,
        grid=(t // tm, 3),
        in_specs=[pl.BlockSpec((tm, W), lambda i, j: (i, j)),
                  pl.BlockSpec((None, 1, FOX_HEAD_DIM), lambda i, j: (j, 0, 0))],
        out_specs=pl.BlockSpec((tm, W), lambda i, j: (i, j)),
        out_shape=jax.ShapeDtypeStruct((t, 3 * W), BF16),
        compiler_params=_params("arbitrary", "arbitrary"),
    )(proj, gains)


def _log_forget_cumsum_body(f_ref, b_ref, o_ref, carry_ref):
    tm = f_ref.shape[0]

    @pl.when(pl.program_id(0) == 0)
    def _():
        carry_ref[...] = jnp.zeros_like(carry_ref)

    row = lax.broadcasted_iota(jnp.int32, (tm, tm), 0)
    col = lax.broadcasted_iota(jnp.int32, (tm, tm), 1)
    log_f = jax.nn.log_sigmoid(f_ref[...] + b_ref[...])
    cum = _dot((col <= row).astype(F32), log_f, HIGHEST) + carry_ref[...]
    o_ref[...] = cum * LOG2E
    carry_ref[...] = cum[tm - 1:tm, :]


def _log_forget_cumsum(f_pre, b_f, tm=512):
    t, nh = f_pre.shape
    return pl.pallas_call(
        _log_forget_cumsum_body,
        grid=(t // tm,),
        in_specs=[pl.BlockSpec((tm, nh), lambda i: (i, 0)), pl.BlockSpec((1, nh), lambda i: (0, 0))],
        out_specs=pl.BlockSpec((tm, nh), lambda i: (i, 0)),
        out_shape=jax.ShapeDtypeStruct((t, nh), F32),
        scratch_shapes=[pltpu.VMEM((1, nh), F32)],
        compiler_params=_params("arbitrary"),
    )(f_pre, b_f.reshape(1, nh))


def _fox_body(q_ref, k_ref, v_ref, g_ref, cq_ref, ck_ref, o_ref, sa_ref, sb_ref, va_ref, *, tq, tk, ts):
    qi = pl.program_id(1)
    n_sub = tq // ts
    subs = range(n_sub)
    rows = [pl.ds(i * ts, ts) for i in subs]

    def fill(dst_ref, kb):
        start = pl.multiple_of(kb * tk, tk)
        for i in subs:
            dst_ref[rows[i], :] = _dot_nt(q_ref[rows[i], :], k_ref[pl.ds(start, tk), :])

    def absorb(src_ref, kb, carry, diagonal):
        start = pl.multiple_of(kb * tk, tk)
        out = []
        for i in subs:
            m_prev, acc = carry[i]
            width = (i + 1) * ts if diagonal else tk
            s = src_ref[rows[i], :width] + cq_ref[rows[i], :] - ck_ref[kb][:, :width]
            if diagonal:
                row = i * ts + lax.broadcasted_iota(jnp.int32, (ts, width), 0)
                col = lax.broadcasted_iota(jnp.int32, (ts, width), 1)
                s = jnp.where(col <= row, s, -jnp.inf)
            m_new = jnp.maximum(m_prev, jnp.max(s, axis=-1, keepdims=True))
            p = jnp.exp2(s - m_new)
            acc = jnp.exp2(m_prev - m_new) * acc + _dot(_bf(p), va_ref[pl.ds(start, width), :])
            out.append((m_new, acc))
        return tuple(out)

    def pair(pi, carry):
        fill(sb_ref, 2 * pi + 1)
        carry = absorb(sa_ref, 2 * pi, carry, False)
        fill(sa_ref, 2 * pi + 2)
        return absorb(sb_ref, 2 * pi + 1, carry, False)

    D = FOX_HEAD_DIM

    @pl.when(qi == 0)
    def _():
        va_ref[:, :D] = v_ref[...]
        va_ref[:, D:] = jnp.ones((va_ref.shape[0], D), va_ref.dtype)

    init = tuple((jnp.full((ts, 1), -jnp.inf, F32), jnp.zeros((ts, 2 * D), F32)) for _ in range(n_sub))
    fill(sa_ref, 0)
    carry = lax.fori_loop(0, qi // 2, pair, init)

    def odd_tail(carry):
        fill(sb_ref, qi)
        carry = absorb(sa_ref, qi - 1, carry, False)
        return absorb(sb_ref, qi, carry, True)

    final = lax.cond(qi % 2 == 1, odd_tail, lambda c: absorb(sa_ref, qi, c, True), carry)
    for i in range(n_sub):
        _, acc = final[i]
        o_ref[rows[i], :] = (acc[:, :D] / acc[:, D:] * jax.nn.sigmoid(g_ref[rows[i], :])).astype(o_ref.dtype)


def _fox_attention(qkv, proj, g_blk0, cum, tq=512):
    t = qkv.shape[0]
    D, H = FOX_HEAD_DIM, FOX_HEADS
    tk = tq
    cum_t = cum.T
    cq = cum_t.reshape(H, t, 1)
    ck = cum_t.reshape(H, t // tk, 1, tk)
    return pl.pallas_call(
        functools.partial(_fox_body, tq=tq, tk=tk, ts=FOX_SUB_TILE),
        grid=(H, t // tq),
        in_specs=[
            pl.BlockSpec((tq, D), lambda h, qi: (qi, h)),
            pl.BlockSpec((t, D), lambda h, qi: (0, H + h)),
            pl.BlockSpec((t, D), lambda h, qi: (0, 2 * H + h)),
            pl.BlockSpec((tq, D), lambda h, qi: (qi, g_blk0 + h)),
            pl.BlockSpec((None, tq, 1), lambda h, qi: (h, qi, 0)),
            pl.BlockSpec((None, t // tk, 1, tk), lambda h, qi: (h, 0, 0, 0)),
        ],
        out_specs=pl.BlockSpec((tq, D), lambda h, qi: (qi, h)),
        out_shape=jax.ShapeDtypeStruct((t, H * D), BF16),
        scratch_shapes=[pltpu.VMEM((tq, tk), F32), pltpu.VMEM((tq, tk), F32), pltpu.VMEM((t, 2 * D), BF16)],
        compiler_params=_params("arbitrary", "arbitrary"),
    )(qkv, qkv, qkv, proj, cq, ck)


def _xattn_body(q_ref, k_ref, v_ref, o_ref):
    s = _dot_nt(q_ref[...], k_ref[...]) * (XATTN_HEAD_DIM ** -0.5)
    p = jnp.exp(s - jnp.max(s, axis=-1, keepdims=True))
    o = _dot(p.astype(BF16), v_ref[...]) / jnp.sum(p, axis=-1, keepdims=True)
    o_ref[...] = o.astype(o_ref.dtype)


def _cross_attention(q, kv, tq=512):
    t = q.shape[0]
    n_mem = kv.shape[0]
    D, H = XATTN_HEAD_DIM, XATTN_HEADS
    return pl.pallas_call(
        _xattn_body,
        grid=(t // tq, H),
        in_specs=[pl.BlockSpec((tq, D), lambda i, h: (i, h)),
                  pl.BlockSpec((n_mem, D), lambda i, h: (0, h)),
                  pl.BlockSpec((n_mem, D), lambda i, h: (0, H + h))],
        out_specs=pl.BlockSpec((tq, D), lambda i, h: (i, h)),
        out_shape=jax.ShapeDtypeStruct((t, H * D), BF16),
        compiler_params=_params("arbitrary", "arbitrary"),
    )(q, kv, kv)


def _swiglu_up_body(x_ref, g_ref, wg_ref, wu_ref, o_ref, xn_ref):
    @pl.when(pl.program_id(1) == 0)
    def _():
        x = x_ref[...]
        xn_ref[...] = _bf(x * lax.rsqrt(jnp.mean(x * x, axis=-1, keepdims=True) + NORM_EPS) * g_ref[...])

    xn = xn_ref[...]
    a = _dot(xn, _bf(wg_ref[...]))
    b = _dot(xn, _bf(wu_ref[...]))
    o_ref[...] = (a * jax.nn.sigmoid(a) * b).astype(o_ref.dtype)


def _swiglu_up(x, norm_g, w_gate, w_up, tm=MM_ROW_TILE, tf=MM_COL_TILE):
    t, d = x.shape
    f = w_gate.shape[1]
    tm = min(tm, t)
    return pl.pallas_call(
        _swiglu_up_body,
        grid=(t // tm, f // tf),
        in_specs=[pl.BlockSpec((tm, d), lambda i, j: (i, 0)),
                  pl.BlockSpec((1, d), lambda i, j: (0, 0)),
                  pl.BlockSpec((d, tf), lambda i, j: (0, j)),
                  pl.BlockSpec((d, tf), lambda i, j: (0, j))],
        out_specs=pl.BlockSpec((tm, tf), lambda i, j: (i, j)),
        out_shape=jax.ShapeDtypeStruct((t, f), BF16),
        scratch_shapes=[pltpu.VMEM((tm, d), BF16)],
        compiler_params=_params("arbitrary", "arbitrary"),
    )(x, norm_g.reshape(1, d), w_gate, w_up)


def _mm_acc_body(x_ref, w_ref, res_ref, o_ref):
    @pl.when(pl.program_id(2) == 0)
    def _():
        o_ref[...] = res_ref[...]

    o_ref[...] += _dot(x_ref[...], _bf(w_ref[...]))


def _mm_acc(x, w, res, tm=MM_ROW_TILE, tn=MM_ROW_TILE, tk=MM_COL_TILE):
    t, k = x.shape
    n = w.shape[1]
    tm = min(tm, t)
    return pl.pallas_call(
        _mm_acc_body,
        grid=(t // tm, n // tn, k // tk),
        in_specs=[pl.BlockSpec((tm, tk), lambda i, j, kk: (i, kk)),
                  pl.BlockSpec((tk, tn), lambda i, j, kk: (kk, j)),
                  pl.BlockSpec((tm, tn), lambda i, j, kk: (i, j))],
        out_specs=pl.BlockSpec((tm, tn), lambda i, j, kk: (i, j)),
        out_shape=jax.ShapeDtypeStruct((t, n), F32),
        compiler_params=_params("arbitrary", "arbitrary", "arbitrary"),
    )(x, w, res)


def _rmsnorm_body(x_ref, g_ref, o_ref):
    x = x_ref[...]
    o_ref[...] = x * lax.rsqrt(jnp.mean(x * x, axis=-1, keepdims=True) + NORM_EPS) * g_ref[...]


def _rmsnorm(x, g, tm=512):
    t, d = x.shape
    return pl.pallas_call(
        _rmsnorm_body,
        grid=(t // tm,),
        in_specs=[pl.BlockSpec((tm, d), lambda i: (i, 0)), pl.BlockSpec((1, d), lambda i: (0, 0))],
        out_specs=pl.BlockSpec((tm, d), lambda i: (i, 0)),
        out_shape=jax.ShapeDtypeStruct((t, d), F32),
        compiler_params=_params("arbitrary"),
    )(x, g.reshape(1, d))


def _pad_cols(w, n=LORA_PAD):
    return jnp.pad(w, [(0, 0)] * (w.ndim - 1) + [(0, n - w.shape[-1])])


def _pack_ab_columns(w):
    a_cols = 3 * RWKV_WIDTH + DECAY_LORA + AAA_LORA + GATE_LORA
    a, b = w[..., :a_cols], w[..., a_cols:]
    o1 = 3 * RWKV_WIDTH
    o2 = o1 + DECAY_LORA
    o3 = o2 + AAA_LORA
    return jnp.concatenate([
        a[..., :o1], b[..., :4 * MLSTM_WIDTH],
        _pad_cols(a[..., o1:o2]), _pad_cols(a[..., o2:o3]), a[..., o3:],
        _pad_cols(b[..., 4 * MLSTM_WIDTH:])], axis=-1)


def _rwkv_mlstm_mixer(x, norm_g, w_in, mu, w0, w_up, a0, a_up, g_up, k_k, k_a, r_k, ln_g, ln_b,
                      conv_qk, b_if, mh_g, w_out):
    proj = _norm_mm(x, _pack_ab_columns(w_in), norm_g=norm_g)
    mu_p = _pack_ab_columns(jnp.concatenate([mu, jnp.zeros((4 * MLSTM_WIDTH + 2 * MLSTM_HEADS,), F32)]))
    o1 = 3 * RWKV_WIDTH
    o_lora = o1 + 4 * MLSTM_WIDTH
    lora_w = 2 * LORA_PAD + GATE_LORA
    y_a = _rwkv7(proj, o_lora, mu_p[:o1], mu_p[o_lora:o_lora + lora_w], w0, w_up, a0, a_up, g_up, k_k, k_a, r_k,
                 ln_g, ln_b, out_dtype=BF16)
    y_b = _mlstm(proj, o1 // MLSTM_WIDTH, o1 // MLSTM_WIDTH + 2, o1 // MLSTM_WIDTH + 3,
                 (o_lora + lora_w) // LORA_PAD, conv_qk, b_if, mh_g, out_dtype=BF16)

    x = _norm_mm(y_a, w_out[:RWKV_WIDTH], res=x)
    return _norm_mm(y_b, w_out[RWKV_WIDTH:], res=x)


def _fox_layer(x, norm_g, w_in, b_f, qn_g, kn_g, w_out):
    W = FOX_HEADS * FOX_HEAD_DIM
    proj = _norm_mm(x, w_in, norm_g=norm_g)
    qkv = _fox_qkv_prep(proj, qn_g, kn_g)
    cum = _log_forget_cumsum(proj[:, 4 * W:], b_f)
    o = _fox_attention(qkv, proj, 3 * FOX_HEADS, cum)
    return _norm_mm(o, w_out, res=x)


def _xattn_layer(x, norm_g, kv, wq, wo):
    q = _norm_mm(x, wq, norm_g=norm_g, out_dtype=BF16)
    return _norm_mm(_cross_attention(q, kv), wo, res=x)


def _moe_experts_body(tile_e_ref, n_valid_ref, tile_rows_ref, x_ref, g_ref, wg_ref, wu_ref, wd_ref, gate_ref,
                      o_ref, xn_ref):
    i = pl.program_id(0)
    j = pl.program_id(1)

    @pl.when(j == 0)
    def _():
        o_ref[...] = jnp.zeros_like(o_ref)
        x = x_ref[...]
        xn_ref[...] = _bf(x * lax.rsqrt(jnp.mean(x * x, axis=-1, keepdims=True) + NORM_EPS) * g_ref[...])

    wg, wu, wd = _bf(wg_ref[...]), _bf(wu_ref[...]), _bf(wd_ref[...])
    n_sub = (tile_rows_ref[i] + MOE_SUB_ROWS - 1) // MOE_SUB_ROWS
    for n in range(1, x_ref.shape[0] // MOE_SUB_ROWS + 1):
        rows = pl.ds(0, n * MOE_SUB_ROWS)

        @pl.when(n_sub == n)
        def _():
            x = xn_ref[rows, :]
            a = _dot(x, wg)
            b = _dot(x, wu)
            h = a * jax.nn.sigmoid(a) * b
            o_ref[rows, :] += _dot(_bf(h), wd)

    @pl.when(j == pl.num_programs(1) - 1)
    def _():
        o_ref[...] = o_ref[...] * gate_ref[...]


def _moe_experts(xg, norm_g, gate_at, tile_e, n_valid, tile_rows, w_gate, w_up, w_down, tm, tf=256):
    p, d = xg.shape
    f = w_gate.shape[2]
    nf = f // tf

    def expert(i, te, nv):
        return te[jnp.minimum(i, nv[0] - 1)]

    def fblock(i, j, nv):
        return jnp.where(i < nv[0], j, nf - 1)

    grid_spec = pltpu.PrefetchScalarGridSpec(
        num_scalar_prefetch=3,
        grid=(p // tm, nf),
        in_specs=[
            pl.BlockSpec((tm, d), lambda i, j, te, nv, tr: (i, 0), pipeline_mode=pl.Buffered(1)),
            pl.BlockSpec((1, d), lambda i, j, te, nv, tr: (0, 0)),
            pl.BlockSpec((None, d, tf), lambda i, j, te, nv, tr: (expert(i, te, nv), 0, fblock(i, j, nv))),
            pl.BlockSpec((None, d, tf), lambda i, j, te, nv, tr: (expert(i, te, nv), 0, fblock(i, j, nv))),
            pl.BlockSpec((None, tf, d), lambda i, j, te, nv, tr: (expert(i, te, nv), fblock(i, j, nv), 0)),
            pl.BlockSpec((tm, 1), lambda i, j, te, nv, tr: (i, 0)),
        ],
        out_specs=pl.BlockSpec((tm, d), lambda i, j, te, nv, tr: (i, 0)),
        scratch_shapes=[pltpu.VMEM((tm, d), BF16)],
    )
    return pl.pallas_call(
        _moe_experts_body,
        grid_spec=grid_spec,
        out_shape=jax.ShapeDtypeStruct((p, d), F32),
        compiler_params=_params("arbitrary", "arbitrary"),
    )(tile_e, n_valid, tile_rows, xg, norm_g.reshape(1, d), w_gate, w_up, w_down, gate_at)


MOE_TILE = 1024
MOE_SUB_ROWS = 256


def _moe_layer(x, norm_g, w_router, w_gate, w_up, w_down):
    t = x.shape[0]
    tm = MOE_TILE
    logits = _norm_mm(x, _pad_cols(w_router), norm_g=norm_g, tm=512, mxu_dtype=F32, precision=HIGHEST)
    top_val, top_idx = lax.top_k(logits[:, :N_EXPERTS], TOP_K)
    gates = jax.nn.softmax(top_val, axis=-1)
    flat_e = top_idx.reshape(-1)
    onehot = (flat_e[:, None] == jnp.arange(N_EXPERTS)[None, :]).astype(jnp.int32)
    ranks = jnp.cumsum(onehot, axis=0) - onehot
    counts = jnp.sum(onehot, axis=0)
    padded = (counts + tm - 1) // tm * tm
    pad_end = jnp.cumsum(padded)
    dest = (pad_end - padded)[flat_e] + jnp.sum(ranks * onehot, axis=1)
    n_rows = TOP_K * t + N_EXPERTS * tm
    tok_at = (jnp.arange(n_rows, dtype=jnp.int32) % t).at[dest].set(jnp.arange(TOP_K * t, dtype=jnp.int32) // TOP_K)
    gate_at = jnp.zeros((n_rows,), F32).at[dest].set(gates.reshape(-1))
    tile_start = jnp.arange(n_rows // tm, dtype=jnp.int32) * tm
    tile_e = jnp.minimum(jnp.sum((tile_start[:, None] >= pad_end[None, :]).astype(jnp.int32), axis=1), N_EXPERTS - 1)
    n_valid = (pad_end[-1:] // tm).astype(jnp.int32)
    routed_end = pad_end - padded + counts
    tile_rows = jnp.clip(routed_end[tile_e] - tile_start, 0, tm).astype(jnp.int32)
    xg = jnp.take(x, tok_at, axis=0, mode="clip")
    y = _moe_experts(xg, norm_g, gate_at.reshape(n_rows, 1), tile_e, n_valid, tile_rows, w_gate, w_up, w_down, tm)
    dest2 = dest.reshape(t, TOP_K)
    return x + jnp.take(y, dest2[:, 0], axis=0, mode="clip") + jnp.take(y, dest2[:, 1], axis=0, mode="clip")


def kernel(x, mem, mem_norm_g, final_norm_g, norm_mix_g, norm_xattn_g, norm_ffn_g,
           xa_wq, xa_wkv, xa_wo,
           ab_w_in, ab_mu, rw_w0, rw_w_up, rw_a0, rw_a_up, rw_g_up, rw_k_k, rw_k_a, rw_r_k,
           rw_ln_g, rw_ln_b, ml_conv, ml_b_if, ml_norm_g, ab_w_out,
           ffn_w_gate, ffn_w_up, ffn_w_down,
           fox_w_in, fox_b_f, fox_qn_g, fox_kn_g, fox_w_out,
           moe_router, moe_w_gate, moe_w_up, moe_w_down):
    bsz, seq, d = x.shape
    assert bsz == 1 and d == D_MODEL and seq % 512 == 0
    xs = x[0]
    mem_s = mem[0]
    depth = norm_mix_g.shape[0]
    for layer in range(depth):
        j = layer // 2
        if layer % 2 == 0:
            xs = _rwkv_mlstm_mixer(xs, norm_mix_g[layer], ab_w_in[j], ab_mu[j], rw_w0[j], rw_w_up[j], rw_a0[j],
                                   rw_a_up[j], rw_g_up[j], rw_k_k[j], rw_k_a[j], rw_r_k[j], rw_ln_g[j],
                                   rw_ln_b[j], ml_conv[j], ml_b_if[j], ml_norm_g[j], ab_w_out[j])
        else:
            xs = _fox_layer(xs, norm_mix_g[layer], fox_w_in[j], fox_b_f[j], fox_qn_g[j], fox_kn_g[j],
                            fox_w_out[j])
        kv = _norm_mm(mem_s, xa_wkv[layer], norm_g=mem_norm_g, tm=mem_s.shape[0], out_dtype=BF16)
        xs = _xattn_layer(xs, norm_xattn_g[layer], kv, xa_wq[layer], xa_wo[layer])
        if layer % 2 == 0:
            hidden = _swiglu_up(xs, norm_ffn_g[layer], ffn_w_gate[j], ffn_w_up[j])
            xs = _mm_acc(hidden, ffn_w_down[j], xs)
        else:
            xs = _moe_layer(xs, norm_ffn_g[layer], moe_router[j], moe_w_gate[j], moe_w_up[j], moe_w_down[j])
    return _rmsnorm(xs, final_norm_g)[None]
```

```python
import functools

import jax
import jax.numpy as jnp
from jax import lax
from jax.experimental import pallas as pl
from jax.experimental.pallas import tpu as pltpu

F32 = jnp.float32
BF16 = jnp.bfloat16
HIGHEST = lax.Precision.HIGHEST

D_MODEL = 2048
CHUNK = 64
NORM_EPS = 1e-6
GN_EPS = 64e-5
RWKV_WIDTH = 1024
RWKV_HEAD_DIM = 64
RWKV_HEADS = 16
RWKV_GROUP = 256
DECAY_LORA = 96
AAA_LORA = 96
GATE_LORA = 256
PREV_ROWS = 8
LORA_PAD = 128
MLSTM_WIDTH = 1024
MLSTM_HEAD_DIM = 256
MLSTM_HEADS = 4
MLSTM_CONV = 4
FOX_HEAD_DIM = 128
FOX_HEADS = 16
XATTN_HEADS = 4
XATTN_HEAD_DIM = 512
D_FF = 5632
N_EXPERTS = 8
TOP_K = 2

VMEM_LIMIT_BYTES = 56 * 1024 * 1024
LOG2E = 1.4426950408889634
FOX_SUB_TILE = 128
MM_ROW_TILE = 1024
MM_COL_TILE = 512


def _params(*semantics):
    return pltpu.CompilerParams(dimension_semantics=semantics, vmem_limit_bytes=VMEM_LIMIT_BYTES)


def _bf(a):
    return a.astype(BF16)


def _dot(a, b, precision=None):
    return jnp.dot(a, b, preferred_element_type=F32, precision=precision)


def _dot_nt(a, b, precision=None):
    return lax.dot_general(a, b, (((1,), (1,)), ((), ())), preferred_element_type=F32, precision=precision)


def _dot_tn(a, b, precision=None):
    return lax.dot_general(a, b, (((0,), (0,)), ((), ())), preferred_element_type=F32, precision=precision)


def _norm_mm_body(*refs, has_norm, has_res, precision):
    it = iter(refs)
    x_ref = next(it)
    g_ref = next(it) if has_norm else None
    w_ref = next(it)
    res_ref = next(it) if has_res else None
    o_ref = next(it)
    xn_ref = next(it)

    @pl.when(pl.program_id(1) == 0)
    def _():
        x = x_ref[...].astype(F32)
        if has_norm:
            x = x * lax.rsqrt(jnp.mean(x * x, axis=-1, keepdims=True) + NORM_EPS) * g_ref[...]
        xn_ref[...] = x.astype(xn_ref.dtype)

    acc = _dot(xn_ref[...], w_ref[...].astype(xn_ref.dtype), precision)
    if has_res:
        acc = acc + res_ref[...]
    o_ref[...] = acc.astype(o_ref.dtype)


def _norm_mm(x, w, *, norm_g=None, res=None, tm=MM_ROW_TILE, tn=MM_COL_TILE, out_dtype=F32, mxu_dtype=BF16,
             precision=None):
    m, k = x.shape
    n = w.shape[1]
    tm, tn = min(tm, m), min(tn, n)
    assert m % tm == 0
    in_specs = [pl.BlockSpec((tm, k), lambda i, j: (i, 0))]
    args = [x]
    if norm_g is not None:
        in_specs.append(pl.BlockSpec((1, k), lambda i, j: (0, 0)))
        args.append(norm_g.reshape(1, k))
    in_specs.append(pl.BlockSpec((k, tn), lambda i, j: (0, j)))
    args.append(w)
    if res is not None:
        in_specs.append(pl.BlockSpec((tm, tn), lambda i, j: (i, j)))
        args.append(res)
    return pl.pallas_call(
        functools.partial(_norm_mm_body, has_norm=norm_g is not None, has_res=res is not None, precision=precision),
        grid=(m // tm, pl.cdiv(n, tn)),
        in_specs=in_specs,
        out_specs=pl.BlockSpec((tm, tn), lambda i, j: (i, j)),
        out_shape=jax.ShapeDtypeStruct((m, n), out_dtype),
        scratch_shapes=[pltpu.VMEM((tm, k), mxu_dtype)],
        compiler_params=_params("arbitrary", "arbitrary"),
    )(*args)


def _rwkv_chunk_body(r_ref, k_ref, v_ref, wlo_ref, alo_ref, glo_ref,
                     rp_ref, kp_ref, vp_ref, wlop_ref, alop_ref, glop_ref,
                     mur_ref, muk_ref, muv_ref, muw_ref, mua_ref, mug_ref,
                     w0_ref, wup_ref, a0_ref, aup_ref, gup_ref, kk_ref, ka_ref, rk_ref,
                     m_ref, n_ref, ry_ref, y0_ref, rkk_ref, vs_ref, g_ref):
    L, N = CHUNK, RWKV_HEAD_DIM
    first_chunk = pl.program_id(0) == 0

    def token_shift(x_ref, prev_ref, mu_ref):
        x = x_ref[...]
        before = jnp.where(first_chunk, 0.0, prev_ref[PREV_ROWS - 1:PREV_ROWS, :])
        first_row = lax.broadcasted_iota(jnp.int32, x.shape, 0) == 0
        prev = jnp.where(first_row, before, pltpu.roll(x, 1, axis=0))
        return x + (prev - x) * mu_ref[...]

    r = token_shift(r_ref, rp_ref, mur_ref)
    k = token_shift(k_ref, kp_ref, muk_ref)
    v = token_shift(v_ref, vp_ref, muv_ref)
    w_lo = token_shift(wlo_ref, wlop_ref, muw_ref)
    a_lo = token_shift(alo_ref, alop_ref, mua_ref)
    g_lo = token_shift(glo_ref, glop_ref, mug_ref)
    log_w = -jax.nn.softplus(-(w0_ref[...] + _dot(jnp.tanh(w_lo), wup_ref[...], HIGHEST))) - 0.5
    lw = -jnp.exp(log_w)
    a = jax.nn.sigmoid(a0_ref[...] + _dot(a_lo, aup_ref[...], HIGHEST))
    g_ref[...] = _dot(jax.nn.sigmoid(g_lo), gup_ref[...], HIGHEST)
    k_mod = k * (1.0 + (a - 1.0) * ka_ref[...])
    rkk_ref[...] = r * k_mod * rk_ref[...]
    vs_ref[...] = v
    kk_raw = k * kk_ref[...]

    row = lax.broadcasted_iota(jnp.int32, (L, L), 0)
    col = lax.broadcasted_iota(jnp.int32, (L, L), 1)
    cum = _dot((col <= row).astype(F32), lw, HIGHEST)
    cum_last = cum[L - 1:L, :]
    g_in = jnp.exp(cum)
    g_prev = jnp.exp(cum - lw)
    g_inv = jnp.exp(-cum)
    g_tail = jnp.exp(cum_last - cum)
    g_last = jnp.exp(cum_last)

    S = RWKV_GROUP
    stack_mask = _head_block_mask(S)
    t_idx = lax.broadcasted_iota(jnp.int32, (L, S), 0)
    s_idx = lax.broadcasted_iota(jnp.int32, (L, S), 1) % N
    incl = s_idx <= t_idx
    strict = s_idx < t_idx
    eye = (s_idx == t_idx).astype(F32)
    blk16 = (t_idx // 16) == (s_idx // 16)
    blk32 = (t_idx // 32) == (s_idx // 32)
    stack = functools.partial(_stack4, mask=stack_mask)
    unstack = functools.partial(_unstack4, mask=stack_mask)

    slabs = [slice(i * S, (i + 1) * S) for i in range(RWKV_WIDTH // S)]
    each = lambda fn, *lists: [fn(*args) for args in zip(*lists)]

    kk_g = each(lambda sl: kk_raw[:, sl], slabs)
    ssq = each(lambda x: _head_sums(x * x, stack_mask), kk_g)
    kk_g = each(lambda x, s2: x / jnp.maximum(jnp.sqrt(s2), 1e-12), kk_g, ssq)
    b_g = each(lambda x, sl: x * a[:, sl], kk_g, slabs)
    rg = each(lambda sl: r[:, sl] * g_in[:, sl], slabs)
    kkg = each(lambda x, sl: x * g_prev[:, sl], kk_g, slabs)
    v_s = each(lambda sl: stack(v[:, sl]), slabs)
    kd_s = each(lambda sl: stack(k_mod[:, sl] * g_inv[:, sl]), slabs)
    bd_s = each(lambda x, sl: stack(x * g_inv[:, sl]), b_g, slabs)
    kdg = each(lambda sl: _bf(k_mod[:, sl] * g_tail[:, sl]), slabs)
    bdg = each(lambda x, sl: _bf(x * g_tail[:, sl]), b_g, slabs)

    lhs = each(lambda x, y: _bf(jnp.concatenate([x, y], axis=0)), kkg, rg)
    with_k = each(_dot_nt, lhs, kd_s)
    with_b = each(_dot_nt, lhs, bd_s)
    a_k = each(lambda x: jnp.where(strict, x[:L], 0.0), with_k)
    a_b = each(lambda x: jnp.where(strict, x[:L], 0.0), with_b)
    ar_k = each(lambda x: _bf(jnp.where(incl, x[L:], 0.0)), with_k)
    ar_b = each(lambda x: _bf(jnp.where(incl, x[L:], 0.0)), with_b)

    pw = each(lambda x: jnp.where(blk16, -x, 0.0), a_b)
    t_inv = each(lambda x: eye + x, pw)
    pw_s = each(stack, pw)
    for _ in range(3):
        pw = each(lambda x, xs: _dot(_bf(x), xs), pw, pw_s)
        pw_s = each(stack, pw)
        t_inv = each(lambda t, xs: t + _dot(_bf(t), xs), t_inv, pw_s)
    for off_mask in (blk32 & (~blk16), ~blk32):
        off_s = each(lambda x: stack(jnp.where(off_mask, x, 0.0)), a_b)
        left = each(lambda t, o: _bf(_dot(_bf(t), o)), t_inv, off_s)
        t_inv = each(lambda t, lt: t - _dot(lt, stack(t)), t_inv, left)
    t_b = each(_bf, t_inv)

    p = each(lambda t, x: _dot(t, stack(x)), t_b, kkg)
    akv = each(lambda x, vs: _dot(_bf(x), vs), a_k, v_s)
    q = each(lambda t, x: _dot(t, stack(x)), t_b, akv)
    for i, sl in enumerate(slabs):
        ry_ref[:, sl] = rg[i] - _dot(ar_b[i], stack(p[i]))
        y0_ref[:, sl] = _dot(ar_k[i], v_s[i]) - _dot(ar_b[i], stack(q[i]))
        m_ref[:, sl] = eye * g_last[:, sl] - unstack(_dot_tn(bdg[i], _bf(p[i])))
        n_ref[:, sl] = unstack(_dot_tn(kdg[i], _bf(v[:, sl])) - _dot_tn(bdg[i], _bf(q[i])))


def _head_block_mask(s):
    row = lax.broadcasted_iota(jnp.int32, (s, s), 0)
    col = lax.broadcasted_iota(jnp.int32, (s, s), 1)
    return (row // RWKV_HEAD_DIM) == (col // RWKV_HEAD_DIM)


def _head_sums(x, mask):
    hi = _bf(x)
    lo = _bf(x - hi.astype(F32))
    both = _dot(jnp.concatenate([hi, lo], axis=0), mask.astype(BF16))
    return both[:x.shape[0]] + both[x.shape[0]:]


def _stack4(x, mask, dtype=BF16):
    reps = mask.shape[0] // x.shape[0]
    return jnp.where(mask, jnp.concatenate([x] * reps, axis=0), 0.0).astype(dtype)


def _unstack4(x, mask):
    L = RWKV_HEAD_DIM
    x = jnp.where(mask, x, 0.0)
    return sum(x[i * L:(i + 1) * L] for i in range(x.shape[0] // L))


def _rwkv_scan_body(m_ref, n_ref, ry_ref, y0_ref, rkk_ref, v_ref, g_ref, lng_ref, lnb_ref, o_ref, h_ref):
    S = RWKV_GROUP
    mask = _head_block_mask(S)

    @pl.when(pl.program_id(0) == 0)
    def _():
        h_ref[...] = jnp.zeros_like(h_ref)

    slabs = [slice(i * S, (i + 1) * S) for i in range(RWKV_WIDTH // S)]
    each = lambda fn, *lists: [fn(*args) for args in zip(*lists)]
    state = [_bf(h_ref[i]) for i in range(len(slabs))]
    y = each(lambda sl, h: _dot(_bf(ry_ref[:, sl]), h) + y0_ref[:, sl], slabs, state)
    for i, sl in enumerate(slabs):
        h_ref[i] = _dot(_stack4(m_ref[:, sl], mask), state[i]) + _stack4(n_ref[:, sl], mask, F32)
    inv_n = 1.0 / RWKV_HEAD_DIM
    mean = each(lambda x: _head_sums(x, mask) * inv_n, y)
    var = each(lambda x, mu: _head_sums(jnp.square(x - mu), mask) * inv_n, y, mean)
    rk = each(lambda sl: _head_sums(rkk_ref[:, sl], mask), slabs)
    for i, sl in enumerate(slabs):
        yn = (y[i] - mean[i]) * lax.rsqrt(var[i] + GN_EPS) * lng_ref[:, sl] + lnb_ref[:, sl]
        o_ref[:, sl] = ((yn + rk[i] * v_ref[:, sl]) * g_ref[:, sl]).astype(o_ref.dtype)


def _rwkv7(proj, lora_col0, mu_rkv, mu_lora, w0, w_up, a0, a_up, g_up, k_k, k_a, r_k, ln_g, ln_b, out_dtype=F32):
    t = proj.shape[0]
    nc = t // CHUNK
    W, L = RWKV_WIDTH, CHUNK
    lo = lora_col0 // LORA_PAD
    go = (lora_col0 + 2 * LORA_PAD) // GATE_LORA
    row_w = lambda i: (i, 0)
    vec = lambda x: x.reshape(1, -1)
    pad_rows = lambda x: jnp.pad(x, ((0, LORA_PAD - x.shape[0]), (0, 0)))
    full = lambda shape: pl.BlockSpec(shape, lambda i: (0,) * len(shape))
    rows_per_prev = L // PREV_ROWS
    cur = lambda width, blk: pl.BlockSpec((L, width), lambda i: (i, blk))
    prev = lambda width, blk: pl.BlockSpec((PREV_ROWS, width), lambda i: (jnp.maximum(i * rows_per_prev - 1, 0), blk))
    columns = [(W, 0), (W, 1), (W, 2), (LORA_PAD, lo), (LORA_PAD, lo + 1), (GATE_LORA, go)]
    mus = [mu_rkv[:W], mu_rkv[W:2 * W], mu_rkv[2 * W:], mu_lora[:LORA_PAD], mu_lora[LORA_PAD:2 * LORA_PAD],
           mu_lora[2 * LORA_PAD:]]
    m, n, ry, y0, rkk, v_s, g = pl.pallas_call(
        _rwkv_chunk_body,
        grid=(nc,),
        in_specs=(
            [cur(w, b) for w, b in columns] + [prev(w, b) for w, b in columns]
            + [full((1, w)) for w, _ in columns]
            + [full((1, W)), full((LORA_PAD, W)), full((1, W)), full((LORA_PAD, W)), full((GATE_LORA, W)),
               full((1, W)), full((1, W)), full((1, W))]),
        out_specs=[pl.BlockSpec((L, W), row_w)] * 7,
        out_shape=[jax.ShapeDtypeStruct((t, W), F32)] * 7,
        compiler_params=_params("arbitrary"),
    )(*([proj] * 12), *[vec(mu) for mu in mus],
      vec(w0), pad_rows(w_up), vec(a0), pad_rows(a_up), g_up, vec(k_k), vec(k_a), vec(r_k))

    return pl.pallas_call(
        _rwkv_scan_body,
        grid=(nc,),
        in_specs=[pl.BlockSpec((L, W), row_w)] * 7 + [full((1, W)), full((1, W))],
        out_specs=pl.BlockSpec((L, W), row_w),
        out_shape=jax.ShapeDtypeStruct((t, W), out_dtype),
        scratch_shapes=[pltpu.VMEM((W // RWKV_GROUP, RWKV_GROUP, RWKV_GROUP), F32)],
        compiler_params=_params("arbitrary"),
    )(m, n, ry, y0, rkk, v_s, g, vec(ln_g), vec(ln_b))


def _mlstm_body(q_ref, k_ref, qp_ref, kp_ref, cw_ref, v_ref, o_ref, gates_ref, bif_ref, ng_ref, out_ref,
                c_ref, n_ref, m_ref):
    L, D = CHUNK, MLSTM_HEAD_DIM
    first_chunk = pl.program_id(0) == 0

    @pl.when(first_chunk)
    def _():
        c_ref[...] = jnp.zeros_like(c_ref)
        n_ref[...] = jnp.zeros_like(n_ref)
        m_ref[...] = jnp.zeros_like(m_ref)

    def conv_silu(x_ref, prev_ref, w):
        x = x_ref[...]
        before = jnp.where(first_chunk, 0.0, prev_ref[...])
        head_rows = lax.broadcasted_iota(jnp.int32, before.shape, 0)
        acc = x * w[MLSTM_CONV - 1:MLSTM_CONV, :]
        for d in range(1, MLSTM_CONV):
            rolled = pltpu.roll(x, d, axis=0)
            top = jnp.where(head_rows < d, pltpu.roll(before, d, axis=0), rolled[:PREV_ROWS])
            shifted = jnp.concatenate([top, rolled[PREV_ROWS:]], axis=0)
            acc = acc + shifted * w[MLSTM_CONV - 1 - d:MLSTM_CONV - d, :]
        return acc * jax.nn.sigmoid(acc)

    q_all = conv_silu(q_ref, qp_ref, cw_ref[:, :MLSTM_WIDTH])
    k_all = conv_silu(k_ref, kp_ref, cw_ref[:, MLSTM_WIDTH:])

    row = lax.broadcasted_iota(jnp.int32, (L, L), 0)
    col = lax.broadcasted_iota(jnp.int32, (L, L), 1)
    incl = col <= row
    eye = (col == row).astype(F32)
    gates = gates_ref[...] + bif_ref[...]
    b_all = _dot(incl.astype(F32), jax.nn.log_sigmoid(gates), HIGHEST)
    to_row = lambda c: jnp.sum(c * eye, axis=0, keepdims=True)

    heads = range(MLSTM_HEADS)
    sls = [slice(h * D, (h + 1) * D) for h in heads]
    q = [q_all[:, sl] * (D ** -0.5) for sl in sls]
    k = [k_all[:, sl] for sl in sls]
    v = [v_ref[:, sl] for sl in sls]
    i_col = [gates[:, h:h + 1] for h in heads]
    b_col = [b_all[:, MLSTM_HEADS + h:MLSTM_HEADS + h + 1] for h in heads]
    m_prev = [m_ref[h] for h in heads]
    c_b = [_bf(c_ref[h]) for h in heads]
    d_mat = [jnp.where(incl, b_col[h] - to_row(b_col[h]) + to_row(i_col[h]), -jnp.inf) for h in heads]
    inter = [b_col[h] + m_prev[h] for h in heads]
    m_row = [jnp.maximum(inter[h], jnp.max(d_mat[h], axis=-1, keepdims=True)) for h in heads]
    w_inter = [jnp.exp(inter[h] - m_row[h]) for h in heads]
    q_b = [_bf(x) for x in q]
    k_b = [_bf(x) for x in k]
    s = [_dot_nt(q_b[h], k_b[h]) * jnp.exp(d_mat[h] - m_row[h]) for h in heads]
    qc = [_dot_nt(q_b[h], c_b[h]) for h in heads]
    sv = [_dot(_bf(s[h]), _bf(v[h])) for h in heads]
    b_last = [b_col[h][L - 1:L, :] for h in heads]
    d_state = [b_last[h] - b_col[h] + i_col[h] for h in heads]
    m_new = [jnp.maximum(b_last[h] + m_prev[h], jnp.max(d_state[h], axis=0, keepdims=True)) for h in heads]
    w_state = [jnp.exp(d_state[h] - m_new[h]) for h in heads]
    w_carry = [jnp.exp(b_last[h] + m_prev[h] - m_new[h]) for h in heads]
    kv = [_dot_tn(_bf(w_state[h] * v[h]), k_b[h]) for h in heads]
    for h in heads:
        n_vec = n_ref[h]
        num = sv[h] + w_inter[h] * qc[h]
        den = jnp.sum(s[h], axis=-1, keepdims=True) + w_inter[h] * jnp.sum(q[h] * n_vec, axis=-1, keepdims=True)
        hh = num / jnp.maximum(jnp.abs(den), jnp.exp(-m_row[h]))
        c_ref[h] = w_carry[h] * c_ref[h] + kv[h]
        n_ref[h] = w_carry[h] * n_vec + jnp.sum(w_state[h] * k[h], axis=0, keepdims=True)
        m_ref[h] = m_new[h]
        hn = hh * lax.rsqrt(jnp.mean(hh * hh, axis=-1, keepdims=True) + NORM_EPS) * ng_ref[:, sls[h]]
        out_ref[:, sls[h]] = (hn * jax.nn.sigmoid(o_ref[:, sls[h]])).astype(out_ref.dtype)


def _mlstm(proj, q_blk, v_blk, o_blk, gates_blk, conv_w, b_if, norm_g, out_dtype=F32):
    t = proj.shape[0]
    L, W = CHUNK, MLSTM_WIDTH
    bif = jnp.pad(b_if, (0, LORA_PAD - b_if.shape[0])).reshape(1, LORA_PAD)
    prev = lambda blk: pl.BlockSpec((PREV_ROWS, W), lambda i: (jnp.maximum(i * (L // PREV_ROWS) - 1, 0), blk))
    return pl.pallas_call(
        _mlstm_body,
        grid=(t // L,),
        in_specs=[
            pl.BlockSpec((L, W), lambda i: (i, q_blk)),
            pl.BlockSpec((L, W), lambda i: (i, q_blk + 1)),
            prev(q_blk), prev(q_blk + 1),
            pl.BlockSpec((MLSTM_CONV, 2 * W), lambda i: (0, 0)),
            pl.BlockSpec((L, W), lambda i: (i, v_blk)),
            pl.BlockSpec((L, W), lambda i: (i, o_blk)),
            pl.BlockSpec((L, LORA_PAD), lambda i: (i, gates_blk)),
            pl.BlockSpec((1, LORA_PAD), lambda i: (0, 0)),
            pl.BlockSpec((1, W), lambda i: (0, 0)),
        ],
        out_specs=pl.BlockSpec((L, W), lambda i: (i, 0)),
        out_shape=jax.ShapeDtypeStruct((t, W), out_dtype),
        scratch_shapes=[pltpu.VMEM((MLSTM_HEADS, MLSTM_HEAD_DIM, MLSTM_HEAD_DIM), F32),
                        pltpu.VMEM((MLSTM_HEADS, 1, MLSTM_HEAD_DIM), F32),
                        pltpu.VMEM((MLSTM_HEADS, 1, 1), F32)],
        compiler_params=_params("arbitrary"),
    )(proj, proj, proj, proj, conv_w, proj, proj, proj, bif, norm_g.reshape(1, W))


def _fox_qkv_prep_body(x_ref, g_ref, o_ref):
    D = FOX_HEAD_DIM

    @pl.when(pl.program_id(1) < 2)
    def _():
        for h in range(FOX_HEADS):
            sl = slice(h * D, (h + 1) * D)
            x = x_ref[:, sl]
            y = x * lax.rsqrt(jnp.mean(x * x, axis=-1, keepdims=True) + NORM_EPS) * g_ref[...]
            o_ref[:, sl] = y.astype(o_ref.dtype)

    @pl.when(pl.program_id(1) == 2)
    def _():
        o_ref[...] = x_ref[...].astype(o_ref.dtype)


def _fox_qkv_prep(proj, qn_g, kn_g, tm=512):
    t = proj.shape[0]
    W = FOX_HEADS * FOX_HEAD_DIM
    gains = jnp.stack([qn_g * (FOX_HEAD_DIM ** -0.5 * LOG2E), kn_g, jnp.ones_like(kn_g)]).reshape(3, 1, FOX_HEAD_DIM)
    return pl.pallas_call(
        _fox_qkv_prep_body,
        grid=(t // tm, 3),
        in_specs=[pl.BlockSpec((tm, W), lambda i, j: (i, j)),
                  pl.BlockSpec((None, 1, FOX_HEAD_DIM), lambda i, j: (j, 0, 0))],
        out_specs=pl.BlockSpec((tm, W), lambda i, j: (i, j)),
        out_shape=jax.ShapeDtypeStruct((t, 3 * W), BF16),
        compiler_params=_params("arbitrary", "arbitrary"),
    )(proj, gains)


def _log_forget_cumsum_body(f_ref, b_ref, o_ref, carry_ref):
    tm = f_ref.shape[0]

    @pl.when(pl.program_id(0) == 0)
    def _():
        carry_ref[...] = jnp.zeros_like(carry_ref)

    row = lax.broadcasted_iota(jnp.int32, (tm, tm), 0)
    col = lax.broadcasted_iota(jnp.int32, (tm, tm), 1)
    log_f = jax.nn.log_sigmoid(f_ref[...] + b_ref[...])
    cum = _dot((col <= row).astype(F32), log_f, HIGHEST) + carry_ref[...]
    o_ref[...] = cum * LOG2E
    carry_ref[...] = cum[tm - 1:tm, :]


def _log_forget_cumsum(f_pre, b_f, tm=512):
    t, nh = f_pre.shape
    return pl.pallas_call(
        _log_forget_cumsum_body,
        grid=(t // tm,),
        in_specs=[pl.BlockSpec((tm, nh), lambda i: (i, 0)), pl.BlockSpec((1, nh), lambda i: (0, 0))],
        out_specs=pl.BlockSpec((tm, nh), lambda i: (i, 0)),
        out_shape=jax.ShapeDtypeStruct((t, nh), F32),
        scratch_shapes=[pltpu.VMEM((1, nh), F32)],
        compiler_params=_params("arbitrary"),
    )(f_pre, b_f.reshape(1, nh))


def _fox_body(q_ref, k_ref, v_ref, g_ref, cq_ref, ck_ref, o_ref, sa_ref, sb_ref, va_ref, acc_ref, *, tq, tk, ts):
    qi = pl.program_id(1)
    n_sub = tq // ts
    subs = range(n_sub)
    rows = [pl.ds(i * ts, ts) for i in subs]

    def fill(dst_ref, kb):
        start = pl.multiple_of(kb * tk, tk)
        for i in subs:
            dst_ref[rows[i], :] = _dot_nt(q_ref[rows[i], :], k_ref[pl.ds(start, tk), :])

    def absorb(src_ref, kb, carry, diagonal):
        start = pl.multiple_of(kb * tk, tk)
        out = []
        for i in subs:
            m_prev = carry[i]
            width = (i + 1) * ts if diagonal else tk
            s = src_ref[rows[i], :width] + cq_ref[rows[i], :] - ck_ref[kb][:, :width]
            if diagonal:
                row = i * ts + lax.broadcasted_iota(jnp.int32, (ts, width), 0)
                col = lax.broadcasted_iota(jnp.int32, (ts, width), 1)
                s = jnp.where(col <= row, s, -jnp.inf)
            m_new = jnp.maximum(m_prev, jnp.max(s, axis=-1, keepdims=True))
            p = jnp.exp2(s - m_new)
            acc_ref[rows[i], :] = (jnp.exp2(m_prev - m_new) * acc_ref[rows[i], :]
                                   + _dot(_bf(p), va_ref[pl.ds(start, width), :]))
            out.append(m_new)
        return tuple(out)

    def pair(pi, carry):
        fill(sb_ref, 2 * pi + 1)
        carry = absorb(sa_ref, 2 * pi, carry, False)
        fill(sa_ref, 2 * pi + 2)
        return absorb(sb_ref, 2 * pi + 1, carry, False)

    D = FOX_HEAD_DIM

    @pl.when(qi == 0)
    def _():
        va_ref[:, :D] = v_ref[...]
        va_ref[:, D:] = jnp.ones((va_ref.shape[0], D), va_ref.dtype)

    acc_ref[...] = jnp.zeros_like(acc_ref)
    init = tuple(jnp.full((ts, 1), -jnp.inf, F32) for _ in range(n_sub))
    fill(sa_ref, 0)
    carry = lax.fori_loop(0, qi // 2, pair, init)

    def odd_tail(carry):
        fill(sb_ref, qi)
        carry = absorb(sa_ref, qi - 1, carry, False)
        return absorb(sb_ref, qi, carry, True)

    lax.cond(qi % 2 == 1, odd_tail, lambda c: absorb(sa_ref, qi, c, True), carry)
    for i in range(n_sub):
        acc = acc_ref[rows[i], :]
        o_ref[rows[i], :] = (acc[:, :D] / acc[:, D:] * jax.nn.sigmoid(g_ref[rows[i], :])).astype(o_ref.dtype)


def _fox_attention(qkv, proj, g_blk0, cum, tq=512):
    t = qkv.shape[0]
    D, H = FOX_HEAD_DIM, FOX_HEADS
    tk = tq
    cum_t = cum.T
    cq = cum_t.reshape(H, t, 1)
    ck = cum_t.reshape(H, t // tk, 1, tk)
    return pl.pallas_call(
        functools.partial(_fox_body, tq=tq, tk=tk, ts=FOX_SUB_TILE),
        grid=(H, t // tq),
        in_specs=[
            pl.BlockSpec((tq, D), lambda h, qi: (qi, h)),
            pl.BlockSpec((t, D), lambda h, qi: (0, H + h)),
            pl.BlockSpec((t, D), lambda h, qi: (0, 2 * H + h)),
            pl.BlockSpec((tq, D), lambda h, qi: (qi, g_blk0 + h)),
            pl.BlockSpec((None, tq, 1), lambda h, qi: (h, qi, 0)),
            pl.BlockSpec((None, t // tk, 1, tk), lambda h, qi: (h, 0, 0, 0)),
        ],
        out_specs=pl.BlockSpec((tq, D), lambda h, qi: (qi, h)),
        out_shape=jax.ShapeDtypeStruct((t, H * D), BF16),
        scratch_shapes=[pltpu.VMEM((tq, tk), F32), pltpu.VMEM((tq, tk), F32), pltpu.VMEM((t, 2 * D), BF16),
                        pltpu.VMEM((tq, 2 * D), F32)],
        compiler_params=_params("arbitrary", "arbitrary"),
    )(qkv, qkv, qkv, proj, cq, ck)


def _xattn_body(q_ref, k_ref, v_ref, o_ref):
    s = _dot_nt(q_ref[...], k_ref[...]) * (XATTN_HEAD_DIM ** -0.5)
    p = jnp.exp(s - jnp.max(s, axis=-1, keepdims=True))
    o = _dot(p.astype(BF16), v_ref[...]) / jnp.sum(p, axis=-1, keepdims=True)
    o_ref[...] = o.astype(o_ref.dtype)


def _cross_attention(q, kv, tq=512):
    t = q.shape[0]
    n_mem = kv.shape[0]
    D, H = XATTN_HEAD_DIM, XATTN_HEADS
    return pl.pallas_call(
        _xattn_body,
        grid=(t // tq, H),
        in_specs=[pl.BlockSpec((tq, D), lambda i, h: (i, h)),
                  pl.BlockSpec((n_mem, D), lambda i, h: (0, h)),
                  pl.BlockSpec((n_mem, D), lambda i, h: (0, H + h))],
        out_specs=pl.BlockSpec((tq, D), lambda i, h: (i, h)),
        out_shape=jax.ShapeDtypeStruct((t, H * D), BF16),
        compiler_params=_params("arbitrary", "arbitrary"),
    )(q, kv, kv)


def _swiglu_up_body(x_ref, g_ref, wg_ref, wu_ref, o_ref, xn_ref):
    @pl.when(pl.program_id(1) == 0)
    def _():
        x = x_ref[...]
        xn_ref[...] = _bf(x * lax.rsqrt(jnp.mean(x * x, axis=-1, keepdims=True) + NORM_EPS) * g_ref[...])

    xn = xn_ref[...]
    a = _dot(xn, _bf(wg_ref[...]))
    b = _dot(xn, _bf(wu_ref[...]))
    o_ref[...] = (a * jax.nn.sigmoid(a) * b).astype(o_ref.dtype)


def _swiglu_up(x, norm_g, w_gate, w_up, tm=MM_ROW_TILE, tf=MM_COL_TILE):
    t, d = x.shape
    f = w_gate.shape[1]
    tm = min(tm, t)
    return pl.pallas_call(
        _swiglu_up_body,
        grid=(t // tm, f // tf),
        in_specs=[pl.BlockSpec((tm, d), lambda i, j: (i, 0)),
                  pl.BlockSpec((1, d), lambda i, j: (0, 0)),
                  pl.BlockSpec((d, tf), lambda i, j: (0, j)),
                  pl.BlockSpec((d, tf), lambda i, j: (0, j))],
        out_specs=pl.BlockSpec((tm, tf), lambda i, j: (i, j)),
        out_shape=jax.ShapeDtypeStruct((t, f), BF16),
        scratch_shapes=[pltpu.VMEM((tm, d), BF16)],
        compiler_params=_params("arbitrary", "arbitrary"),
    )(x, norm_g.reshape(1, d), w_gate, w_up)


def _mm_acc_body(x_ref, w_ref, res_ref, o_ref):
    @pl.when(pl.program_id(2) == 0)
    def _():
        o_ref[...] = res_ref[...]

    o_ref[...] += _dot(x_ref[...], _bf(w_ref[...]))


def _mm_acc(x, w, res, tm=MM_ROW_TILE, tn=MM_ROW_TILE, tk=MM_COL_TILE):
    t, k = x.shape
    n = w.shape[1]
    tm = min(tm, t)
    return pl.pallas_call(
        _mm_acc_body,
        grid=(t // tm, n // tn, k // tk),
        in_specs=[pl.BlockSpec((tm, tk), lambda i, j, kk: (i, kk)),
                  pl.BlockSpec((tk, tn), lambda i, j, kk: (kk, j)),
                  pl.BlockSpec((tm, tn), lambda i, j, kk: (i, j))],
        out_specs=pl.BlockSpec((tm, tn), lambda i, j, kk: (i, j)),
        out_shape=jax.ShapeDtypeStruct((t, n), F32),
        compiler_params=_params("arbitrary", "arbitrary", "arbitrary"),
    )(x, w, res)


def _rmsnorm_body(x_ref, g_ref, o_ref):
    x = x_ref[...]
    o_ref[...] = x * lax.rsqrt(jnp.mean(x * x, axis=-1, keepdims=True) + NORM_EPS) * g_ref[...]


def _rmsnorm(x, g, tm=512):
    t, d = x.shape
    return pl.pallas_call(
        _rmsnorm_body,
        grid=(t // tm,),
        in_specs=[pl.BlockSpec((tm, d), lambda i: (i, 0)), pl.BlockSpec((1, d), lambda i: (0, 0))],
        out_specs=pl.BlockSpec((tm, d), lambda i: (i, 0)),
        out_shape=jax.ShapeDtypeStruct((t, d), F32),
        compiler_params=_params("arbitrary"),
    )(x, g.reshape(1, d))


def _pad_cols(w, n=LORA_PAD):
    return jnp.pad(w, [(0, 0)] * (w.ndim - 1) + [(0, n - w.shape[-1])])


def _pack_ab_columns(w):
    a_cols = 3 * RWKV_WIDTH + DECAY_LORA + AAA_LORA + GATE_LORA
    a, b = w[..., :a_cols], w[..., a_cols:]
    o1 = 3 * RWKV_WIDTH
    o2 = o1 + DECAY_LORA
    o3 = o2 + AAA_LORA
    return jnp.concatenate([
        a[..., :o1], b[..., :4 * MLSTM_WIDTH],
        _pad_cols(a[..., o1:o2]), _pad_cols(a[..., o2:o3]), a[..., o3:],
        _pad_cols(b[..., 4 * MLSTM_WIDTH:])], axis=-1)


def _rwkv_mlstm_mixer(x, norm_g, w_in, mu, w0, w_up, a0, a_up, g_up, k_k, k_a, r_k, ln_g, ln_b,
                      conv_qk, b_if, mh_g, w_out):
    proj = _norm_mm(x, _pack_ab_columns(w_in), norm_g=norm_g)
    mu_p = _pack_ab_columns(jnp.concatenate([mu, jnp.zeros((4 * MLSTM_WIDTH + 2 * MLSTM_HEADS,), F32)]))
    o1 = 3 * RWKV_WIDTH
    o_lora = o1 + 4 * MLSTM_WIDTH
    lora_w = 2 * LORA_PAD + GATE_LORA
    y_a = _rwkv7(proj, o_lora, mu_p[:o1], mu_p[o_lora:o_lora + lora_w], w0, w_up, a0, a_up, g_up, k_k, k_a, r_k,
                 ln_g, ln_b, out_dtype=BF16)
    y_b = _mlstm(proj, o1 // MLSTM_WIDTH, o1 // MLSTM_WIDTH + 2, o1 // MLSTM_WIDTH + 3,
                 (o_lora + lora_w) // LORA_PAD, conv_qk, b_if, mh_g, out_dtype=BF16)

    x = _norm_mm(y_a, w_out[:RWKV_WIDTH], res=x)
    return _norm_mm(y_b, w_out[RWKV_WIDTH:], res=x)


def _fox_layer(x, norm_g, w_in, b_f, qn_g, kn_g, w_out):
    W = FOX_HEADS * FOX_HEAD_DIM
    proj = _norm_mm(x, w_in, norm_g=norm_g)
    qkv = _fox_qkv_prep(proj, qn_g, kn_g)
    cum = _log_forget_cumsum(proj[:, 4 * W:], b_f)
    o = _fox_attention(qkv, proj, 3 * FOX_HEADS, cum)
    return _norm_mm(o, w_out, res=x)


def _xattn_layer(x, norm_g, kv, wq, wo):
    q = _norm_mm(x, wq, norm_g=norm_g, out_dtype=BF16)
    return _norm_mm(_cross_attention(q, kv), wo, res=x)


def _moe_experts_body(tile_e_ref, n_valid_ref, tile_rows_ref, x_ref, g_ref, wg_ref, wu_ref, wd_ref, gate_ref,
                      o_ref, xn_ref):
    i = pl.program_id(0)
    j = pl.program_id(1)

    @pl.when(j == 0)
    def _():
        o_ref[...] = jnp.zeros_like(o_ref)
        x = x_ref[...]
        xn_ref[...] = _bf(x * lax.rsqrt(jnp.mean(x * x, axis=-1, keepdims=True) + NORM_EPS) * g_ref[...])

    wg, wu, wd = _bf(wg_ref[...]), _bf(wu_ref[...]), _bf(wd_ref[...])
    n_sub = (tile_rows_ref[i] + MOE_SUB_ROWS - 1) // MOE_SUB_ROWS
    for n in range(1, x_ref.shape[0] // MOE_SUB_ROWS + 1):
        rows = pl.ds(0, n * MOE_SUB_ROWS)

        @pl.when(n_sub == n)
        def _():
            x = xn_ref[rows, :]
            a = _dot(x, wg)
            b = _dot(x, wu)
            h = a * jax.nn.sigmoid(a) * b
            o_ref[rows, :] += _dot(_bf(h), wd)

    @pl.when(j == pl.num_programs(1) - 1)
    def _():
        o_ref[...] = o_ref[...] * gate_ref[...]


def _moe_experts(xg, norm_g, gate_at, tile_e, n_valid, tile_rows, w_gate, w_up, w_down, tm, tf=256):
    p, d = xg.shape
    f = w_gate.shape[2]
    nf = f // tf

    def expert(i, te, nv):
        return te[jnp.minimum(i, nv[0] - 1)]

    def fblock(i, j, nv):
        return jnp.where(i < nv[0], j, nf - 1)

    grid_spec = pltpu.PrefetchScalarGridSpec(
        num_scalar_prefetch=3,
        grid=(p // tm, nf),
        in_specs=[
            pl.BlockSpec((tm, d), lambda i, j, te, nv, tr: (i, 0), pipeline_mode=pl.Buffered(1)),
            pl.BlockSpec((1, d), lambda i, j, te, nv, tr: (0, 0)),
            pl.BlockSpec((None, d, tf), lambda i, j, te, nv, tr: (expert(i, te, nv), 0, fblock(i, j, nv))),
            pl.BlockSpec((None, d, tf), lambda i, j, te, nv, tr: (expert(i, te, nv), 0, fblock(i, j, nv))),
            pl.BlockSpec((None, tf, d), lambda i, j, te, nv, tr: (expert(i, te, nv), fblock(i, j, nv), 0)),
            pl.BlockSpec((tm, 1), lambda i, j, te, nv, tr: (i, 0)),
        ],
        out_specs=pl.BlockSpec((tm, d), lambda i, j, te, nv, tr: (i, 0)),
        scratch_shapes=[pltpu.VMEM((tm, d), BF16)],
    )
    return pl.pallas_call(
        _moe_experts_body,
        grid_spec=grid_spec,
        out_shape=jax.ShapeDtypeStruct((p, d), F32),
        compiler_params=_params("arbitrary", "arbitrary"),
    )(tile_e, n_valid, tile_rows, xg, norm_g.reshape(1, d), w_gate, w_up, w_down, gate_at)


MOE_TILE = 1024
MOE_SUB_ROWS = 256


def _moe_layer(x, norm_g, w_router, w_gate, w_up, w_down):
    t = x.shape[0]
    tm = MOE_TILE
    logits = _norm_mm(x, _pad_cols(w_router), norm_g=norm_g, tm=512, mxu_dtype=F32, precision=HIGHEST)
    top_val, top_idx = lax.top_k(logits[:, :N_EXPERTS], TOP_K)
    gates = jax.nn.softmax(top_val, axis=-1)
    flat_e = top_idx.reshape(-1)
    onehot = (flat_e[:, None] == jnp.arange(N_EXPERTS)[None, :]).astype(jnp.int32)
    ranks = jnp.cumsum(onehot, axis=0) - onehot
    counts = jnp.sum(onehot, axis=0)
    padded = (counts + tm - 1) // tm * tm
    pad_end = jnp.cumsum(padded)
    dest = (pad_end - padded)[flat_e] + jnp.sum(ranks * onehot, axis=1)
    n_rows = TOP_K * t + N_EXPERTS * tm
    tok_at = (jnp.arange(n_rows, dtype=jnp.int32) % t).at[dest].set(jnp.arange(TOP_K * t, dtype=jnp.int32) // TOP_K)
    gate_at = jnp.zeros((n_rows,), F32).at[dest].set(gates.reshape(-1))
    tile_start = jnp.arange(n_rows // tm, dtype=jnp.int32) * tm
    tile_e = jnp.minimum(jnp.sum((tile_start[:, None] >= pad_end[None, :]).astype(jnp.int32), axis=1), N_EXPERTS - 1)
    n_valid = (pad_end[-1:] // tm).astype(jnp.int32)
    routed_end = pad_end - padded + counts
    tile_rows = jnp.clip(routed_end[tile_e] - tile_start, 0, tm).astype(jnp.int32)
    xg = jnp.take(x, tok_at, axis=0, mode="clip")
    y = _moe_experts(xg, norm_g, gate_at.reshape(n_rows, 1), tile_e, n_valid, tile_rows, w_gate, w_up, w_down, tm)
    dest2 = dest.reshape(t, TOP_K)
    return x + jnp.take(y, dest2[:, 0], axis=0, mode="clip") + jnp.take(y, dest2[:, 1], axis=0, mode="clip")


def kernel(x, mem, mem_norm_g, final_norm_g, norm_mix_g, norm_xattn_g, norm_ffn_g,
           xa_wq, xa_wkv, xa_wo,
           ab_w_in, ab_mu, rw_w0, rw_w_up, rw_a0, rw_a_up, rw_g_up, rw_k_k, rw_k_a, rw_r_k,
           rw_ln_g, rw_ln_b, ml_conv, ml_b_if, ml_norm_g, ab_w_out,
           ffn_w_gate, ffn_w_up, ffn_w_down,
           fox_w_in, fox_b_f, fox_qn_g, fox_kn_g, fox_w_out,
           moe_router, moe_w_gate, moe_w_up, moe_w_down):
    bsz, seq, d = x.shape
    assert bsz == 1 and d == D_MODEL and seq % 512 == 0
    xs = x[0]
    mem_s = mem[0]
    depth = norm_mix_g.shape[0]
    for layer in range(depth):
        j = layer // 2
        if layer % 2 == 0:
            xs = _rwkv_mlstm_mixer(xs, norm_mix_g[layer], ab_w_in[j], ab_mu[j], rw_w0[j], rw_w_up[j], rw_a0[j],
                                   rw_a_up[j], rw_g_up[j], rw_k_k[j], rw_k_a[j], rw_r_k[j], rw_ln_g[j],
                                   rw_ln_b[j], ml_conv[j], ml_b_if[j], ml_norm_g[j], ab_w_out[j])
        else:
            xs = _fox_layer(xs, norm_mix_g[layer], fox_w_in[j], fox_b_f[j], fox_qn_g[j], fox_kn_g[j],
                            fox_w_out[j])
        kv = _norm_mm(mem_s, xa_wkv[layer], norm_g=mem_norm_g, tm=mem_s.shape[0], out_dtype=BF16)
        xs = _xattn_layer(xs, norm_xattn_g[layer], kv, xa_wq[layer], xa_wo[layer])
        if layer % 2 == 0:
            hidden = _swiglu_up(xs, norm_ffn_g[layer], ffn_w_gate[j], ffn_w_up[j])
            xs = _mm_acc(hidden, ffn_w_down[j], xs)
        else:
            xs = _moe_layer(xs, norm_ffn_g[layer], moe_router[j], moe_w_gate[j], moe_w_up[j], moe_w_down[j])
    return _rmsnorm(xs, final_norm_g)[None]
```

```python
import functools

import jax
import jax.numpy as jnp
from jax import lax
from jax.experimental import pallas as pl
from jax.experimental.pallas import tpu as pltpu

F32 = jnp.float32
BF16 = jnp.bfloat16
HIGHEST = lax.Precision.HIGHEST

D_MODEL = 2048
CHUNK = 64
NORM_EPS = 1e-6
GN_EPS = 64e-5
RWKV_WIDTH = 1024
RWKV_HEAD_DIM = 64
RWKV_HEADS = 16
RWKV_GROUP = 256
DECAY_LORA = 96
AAA_LORA = 96
GATE_LORA = 256
PREV_ROWS = 8
LORA_PAD = 128
MLSTM_WIDTH = 1024
MLSTM_HEAD_DIM = 256
MLSTM_HEADS = 4
MLSTM_CONV = 4
FOX_HEAD_DIM = 128
FOX_HEADS = 16
XATTN_HEADS = 4
XATTN_HEAD_DIM = 512
D_FF = 5632
N_EXPERTS = 8
TOP_K = 2

VMEM_LIMIT_BYTES = 56 * 1024 * 1024
LOG2E = 1.4426950408889634
FOX_SUB_TILE = 128
MM_ROW_TILE = 1024
MM_COL_TILE = 512


def _params(*semantics):
    return pltpu.CompilerParams(dimension_semantics=semantics, vmem_limit_bytes=VMEM_LIMIT_BYTES)


def _bf(a):
    return a.astype(BF16)


def _dot(a, b, precision=None):
    return jnp.dot(a, b, preferred_element_type=F32, precision=precision)


def _dot_split(a, b):
    a_hi, b_hi = _bf(a), _bf(b)
    a_lo, b_lo = _bf(a - a_hi.astype(F32)), _bf(b - b_hi.astype(F32))
    return _dot(a_hi, b_hi) + (_dot(a_hi, b_lo) + _dot(a_lo, b_hi))


def _dot_nt(a, b, precision=None):
    return lax.dot_general(a, b, (((1,), (1,)), ((), ())), preferred_element_type=F32, precision=precision)


def _dot_tn(a, b, precision=None):
    return lax.dot_general(a, b, (((0,), (0,)), ((), ())), preferred_element_type=F32, precision=precision)


def _norm_mm_body(*refs, has_norm, has_res, precision):
    it = iter(refs)
    x_ref = next(it)
    g_ref = next(it) if has_norm else None
    w_ref = next(it)
    res_ref = next(it) if has_res else None
    o_ref = next(it)
    xn_ref = next(it)

    @pl.when(pl.program_id(1) == 0)
    def _():
        x = x_ref[...].astype(F32)
        if has_norm:
            x = x * lax.rsqrt(jnp.mean(x * x, axis=-1, keepdims=True) + NORM_EPS) * g_ref[...]
        xn_ref[...] = x.astype(xn_ref.dtype)

    acc = _dot(xn_ref[...], w_ref[...].astype(xn_ref.dtype), precision)
    if has_res:
        acc = acc + res_ref[...]
    o_ref[...] = acc.astype(o_ref.dtype)


def _norm_mm(x, w, *, norm_g=None, res=None, tm=MM_ROW_TILE, tn=MM_COL_TILE, out_dtype=F32, mxu_dtype=BF16,
             precision=None):
    m, k = x.shape
    n = w.shape[1]
    tm, tn = min(tm, m), min(tn, n)
    assert m % tm == 0
    in_specs = [pl.BlockSpec((tm, k), lambda i, j: (i, 0))]
    args = [x]
    if norm_g is not None:
        in_specs.append(pl.BlockSpec((1, k), lambda i, j: (0, 0)))
        args.append(norm_g.reshape(1, k))
    in_specs.append(pl.BlockSpec((k, tn), lambda i, j: (0, j)))
    args.append(w)
    if res is not None:
        in_specs.append(pl.BlockSpec((tm, tn), lambda i, j: (i, j)))
        args.append(res)
    return pl.pallas_call(
        functools.partial(_norm_mm_body, has_norm=norm_g is not None, has_res=res is not None, precision=precision),
        grid=(m // tm, pl.cdiv(n, tn)),
        in_specs=in_specs,
        out_specs=pl.BlockSpec((tm, tn), lambda i, j: (i, j)),
        out_shape=jax.ShapeDtypeStruct((m, n), out_dtype),
        scratch_shapes=[pltpu.VMEM((tm, k), mxu_dtype)],
        compiler_params=_params("arbitrary", "arbitrary"),
    )(*args)


def _rwkv_chunk_body(r_ref, k_ref, v_ref, wlo_ref, alo_ref, glo_ref,
                     rp_ref, kp_ref, vp_ref, wlop_ref, alop_ref, glop_ref,
                     mur_ref, muk_ref, muv_ref, muw_ref, mua_ref, mug_ref,
                     w0_ref, wup_ref, a0_ref, aup_ref, gup_ref, kk_ref, ka_ref, rk_ref,
                     m_ref, n_ref, ry_ref, y0_ref, rkk_ref, vs_ref, g_ref):
    L, N = CHUNK, RWKV_HEAD_DIM
    first_chunk = pl.program_id(0) == 0

    def token_shift(x_ref, prev_ref, mu_ref):
        x = x_ref[...]
        before = jnp.where(first_chunk, 0.0, prev_ref[PREV_ROWS - 1:PREV_ROWS, :])
        first_row = lax.broadcasted_iota(jnp.int32, x.shape, 0) == 0
        prev = jnp.where(first_row, before, pltpu.roll(x, 1, axis=0))
        return x + (prev - x) * mu_ref[...]

    r = token_shift(r_ref, rp_ref, mur_ref)
    k = token_shift(k_ref, kp_ref, muk_ref)
    v = token_shift(v_ref, vp_ref, muv_ref)
    w_lo = token_shift(wlo_ref, wlop_ref, muw_ref)
    a_lo = token_shift(alo_ref, alop_ref, mua_ref)
    g_lo = token_shift(glo_ref, glop_ref, mug_ref)
    log_w = -jax.nn.softplus(-(w0_ref[...] + _dot_split(jnp.tanh(w_lo), wup_ref[...]))) - 0.5
    lw = -jnp.exp(log_w)
    a = jax.nn.sigmoid(a0_ref[...] + _dot_split(a_lo, aup_ref[...]))
    g_ref[...] = _dot_split(jax.nn.sigmoid(g_lo), gup_ref[...])
    k_mod = k * (1.0 + (a - 1.0) * ka_ref[...])
    rkk_ref[...] = r * k_mod * rk_ref[...]
    vs_ref[...] = v
    kk_raw = k * kk_ref[...]

    row = lax.broadcasted_iota(jnp.int32, (L, L), 0)
    col = lax.broadcasted_iota(jnp.int32, (L, L), 1)
    cum = _dot((col <= row).astype(F32), lw, HIGHEST)
    cum_last = cum[L - 1:L, :]
    g_in = jnp.exp(cum)
    g_prev = jnp.exp(cum - lw)
    g_inv = jnp.exp(-cum)
    g_tail = jnp.exp(cum_last - cum)
    g_last = jnp.exp(cum_last)

    S = RWKV_GROUP
    stack_mask = _head_block_mask(S)
    t_idx = lax.broadcasted_iota(jnp.int32, (L, S), 0)
    s_idx = lax.broadcasted_iota(jnp.int32, (L, S), 1) % N
    incl = s_idx <= t_idx
    strict = s_idx < t_idx
    eye = (s_idx == t_idx).astype(F32)
    blk16 = (t_idx // 16) == (s_idx // 16)
    blk32 = (t_idx // 32) == (s_idx // 32)
    stack = functools.partial(_stack4, mask=stack_mask)
    unstack = functools.partial(_unstack4, mask=stack_mask)

    slabs = [slice(i * S, (i + 1) * S) for i in range(RWKV_WIDTH // S)]
    each = lambda fn, *lists: [fn(*args) for args in zip(*lists)]

    kk_g = each(lambda sl: kk_raw[:, sl], slabs)
    ssq = each(lambda x: _head_sums(x * x, stack_mask), kk_g)
    kk_g = each(lambda x, s2: x / jnp.maximum(jnp.sqrt(s2), 1e-12), kk_g, ssq)
    b_g = each(lambda x, sl: x * a[:, sl], kk_g, slabs)
    rg = each(lambda sl: r[:, sl] * g_in[:, sl], slabs)
    kkg = each(lambda x, sl: x * g_prev[:, sl], kk_g, slabs)
    v_s = each(lambda sl: stack(v[:, sl]), slabs)
    kd_s = each(lambda sl: stack(k_mod[:, sl] * g_inv[:, sl]), slabs)
    bd_s = each(lambda x, sl: stack(x * g_inv[:, sl]), b_g, slabs)
    kdg = each(lambda sl: _bf(k_mod[:, sl] * g_tail[:, sl]), slabs)
    bdg = each(lambda x, sl: _bf(x * g_tail[:, sl]), b_g, slabs)

    lhs = each(lambda x, y: _bf(jnp.concatenate([x, y], axis=0)), kkg, rg)
    with_k = each(_dot_nt, lhs, kd_s)
    with_b = each(_dot_nt, lhs, bd_s)
    a_k = each(lambda x: jnp.where(strict, x[:L], 0.0), with_k)
    a_b = each(lambda x: jnp.where(strict, x[:L], 0.0), with_b)
    ar_k = each(lambda x: _bf(jnp.where(incl, x[L:], 0.0)), with_k)
    ar_b = each(lambda x: _bf(jnp.where(incl, x[L:], 0.0)), with_b)

    pw = each(lambda x: jnp.where(blk16, -x, 0.0), a_b)
    t_inv = each(lambda x: eye + x, pw)
    pw_s = each(stack, pw)
    for _ in range(3):
        pw = each(lambda x, xs: _dot(_bf(x), xs), pw, pw_s)
        pw_s = each(stack, pw)
        t_inv = each(lambda t, xs: t + _dot(_bf(t), xs), t_inv, pw_s)
    for off_mask in (blk32 & (~blk16), ~blk32):
        off_s = each(lambda x: stack(jnp.where(off_mask, x, 0.0)), a_b)
        left = each(lambda t, o: _bf(_dot(_bf(t), o)), t_inv, off_s)
        t_inv = each(lambda t, lt: t - _dot(lt, stack(t)), t_inv, left)
    t_b = each(_bf, t_inv)

    p = each(lambda t, x: _dot(t, stack(x)), t_b, kkg)
    akv = each(lambda x, vs: _dot(_bf(x), vs), a_k, v_s)
    q = each(lambda t, x: _dot(t, stack(x)), t_b, akv)
    for i, sl in enumerate(slabs):
        ry_ref[:, sl] = rg[i] - _dot(ar_b[i], stack(p[i]))
        y0_ref[:, sl] = _dot(ar_k[i], v_s[i]) - _dot(ar_b[i], stack(q[i]))
        m_ref[:, sl] = eye * g_last[:, sl] - unstack(_dot_tn(bdg[i], _bf(p[i])))
        n_ref[:, sl] = unstack(_dot_tn(kdg[i], _bf(v[:, sl])) - _dot_tn(bdg[i], _bf(q[i])))


def _head_block_mask(s):
    row = lax.broadcasted_iota(jnp.int32, (s, s), 0)
    col = lax.broadcasted_iota(jnp.int32, (s, s), 1)
    return (row // RWKV_HEAD_DIM) == (col // RWKV_HEAD_DIM)


def _head_sums(x, mask):
    hi = _bf(x)
    lo = _bf(x - hi.astype(F32))
    both = _dot(jnp.concatenate([hi, lo], axis=0), mask.astype(BF16))
    return both[:x.shape[0]] + both[x.shape[0]:]


def _stack4(x, mask, dtype=BF16):
    reps = mask.shape[0] // x.shape[0]
    return jnp.where(mask, jnp.concatenate([x] * reps, axis=0), 0.0).astype(dtype)


def _unstack4(x, mask):
    L = RWKV_HEAD_DIM
    x = jnp.where(mask, x, 0.0)
    return sum(x[i * L:(i + 1) * L] for i in range(x.shape[0] // L))


def _rwkv_scan_body(m_ref, n_ref, ry_ref, y0_ref, rkk_ref, v_ref, g_ref, lng_ref, lnb_ref, o_ref, h_ref):
    S = RWKV_GROUP
    mask = _head_block_mask(S)

    @pl.when(pl.program_id(0) == 0)
    def _():
        h_ref[...] = jnp.zeros_like(h_ref)

    slabs = [slice(i * S, (i + 1) * S) for i in range(RWKV_WIDTH // S)]
    each = lambda fn, *lists: [fn(*args) for args in zip(*lists)]
    state = [_bf(h_ref[i]) for i in range(len(slabs))]
    y = each(lambda sl, h: _dot(_bf(ry_ref[:, sl]), h) + y0_ref[:, sl], slabs, state)
    for i, sl in enumerate(slabs):
        h_ref[i] = _dot(_stack4(m_ref[:, sl], mask), state[i]) + _stack4(n_ref[:, sl], mask, F32)
    inv_n = 1.0 / RWKV_HEAD_DIM
    mean = each(lambda x: _head_sums(x, mask) * inv_n, y)
    var = each(lambda x, mu: _head_sums(jnp.square(x - mu), mask) * inv_n, y, mean)
    rk = each(lambda sl: _head_sums(rkk_ref[:, sl], mask), slabs)
    for i, sl in enumerate(slabs):
        yn = (y[i] - mean[i]) * lax.rsqrt(var[i] + GN_EPS) * lng_ref[:, sl] + lnb_ref[:, sl]
        o_ref[:, sl] = ((yn + rk[i] * v_ref[:, sl]) * g_ref[:, sl]).astype(o_ref.dtype)


def _rwkv7(proj, lora_col0, mu_rkv, mu_lora, w0, w_up, a0, a_up, g_up, k_k, k_a, r_k, ln_g, ln_b, out_dtype=F32):
    t = proj.shape[0]
    nc = t // CHUNK
    W, L = RWKV_WIDTH, CHUNK
    lo = lora_col0 // LORA_PAD
    go = (lora_col0 + 2 * LORA_PAD) // GATE_LORA
    row_w = lambda i: (i, 0)
    vec = lambda x: x.reshape(1, -1)
    pad_rows = lambda x: jnp.pad(x, ((0, LORA_PAD - x.shape[0]), (0, 0)))
    full = lambda shape: pl.BlockSpec(shape, lambda i: (0,) * len(shape))
    rows_per_prev = L // PREV_ROWS
    cur = lambda width, blk: pl.BlockSpec((L, width), lambda i: (i, blk))
    prev = lambda width, blk: pl.BlockSpec((PREV_ROWS, width), lambda i: (jnp.maximum(i * rows_per_prev - 1, 0), blk))
    columns = [(W, 0), (W, 1), (W, 2), (LORA_PAD, lo), (LORA_PAD, lo + 1), (GATE_LORA, go)]
    mus = [mu_rkv[:W], mu_rkv[W:2 * W], mu_rkv[2 * W:], mu_lora[:LORA_PAD], mu_lora[LORA_PAD:2 * LORA_PAD],
           mu_lora[2 * LORA_PAD:]]
    m, n, ry, y0, rkk, v_s, g = pl.pallas_call(
        _rwkv_chunk_body,
        grid=(nc,),
        in_specs=(
            [cur(w, b) for w, b in columns] + [prev(w, b) for w, b in columns]
            + [full((1, w)) for w, _ in columns]
            + [full((1, W)), full((LORA_PAD, W)), full((1, W)), full((LORA_PAD, W)), full((GATE_LORA, W)),
               full((1, W)), full((1, W)), full((1, W))]),
        out_specs=[pl.BlockSpec((L, W), row_w)] * 7,
        out_shape=[jax.ShapeDtypeStruct((t, W), F32)] * 7,
        compiler_params=_params("arbitrary"),
    )(*([proj] * 12), *[vec(mu) for mu in mus],
      vec(w0), pad_rows(w_up), vec(a0), pad_rows(a_up), g_up, vec(k_k), vec(k_a), vec(r_k))

    return pl.pallas_call(
        _rwkv_scan_body,
        grid=(nc,),
        in_specs=[pl.BlockSpec((L, W), row_w)] * 7 + [full((1, W)), full((1, W))],
        out_specs=pl.BlockSpec((L, W), row_w),
        out_shape=jax.ShapeDtypeStruct((t, W), out_dtype),
        scratch_shapes=[pltpu.VMEM((W // RWKV_GROUP, RWKV_GROUP, RWKV_GROUP), F32)],
        compiler_params=_params("arbitrary"),
    )(m, n, ry, y0, rkk, v_s, g, vec(ln_g), vec(ln_b))


def _mlstm_body(q_ref, k_ref, qp_ref, kp_ref, cw_ref, v_ref, o_ref, gates_ref, bif_ref, ng_ref, out_ref,
                c_ref, n_ref, m_ref):
    L, D = CHUNK, MLSTM_HEAD_DIM
    first_chunk = pl.program_id(0) == 0

    @pl.when(first_chunk)
    def _():
        c_ref[...] = jnp.zeros_like(c_ref)
        n_ref[...] = jnp.zeros_like(n_ref)
        m_ref[...] = jnp.zeros_like(m_ref)

    def conv_silu(x_ref, prev_ref, w):
        x = x_ref[...]
        before = jnp.where(first_chunk, 0.0, prev_ref[...])
        head_rows = lax.broadcasted_iota(jnp.int32, before.shape, 0)
        acc = x * w[MLSTM_CONV - 1:MLSTM_CONV, :]
        for d in range(1, MLSTM_CONV):
            rolled = pltpu.roll(x, d, axis=0)
            top = jnp.where(head_rows < d, pltpu.roll(before, d, axis=0), rolled[:PREV_ROWS])
            shifted = jnp.concatenate([top, rolled[PREV_ROWS:]], axis=0)
            acc = acc + shifted * w[MLSTM_CONV - 1 - d:MLSTM_CONV - d, :]
        return acc * jax.nn.sigmoid(acc)

    q_all = conv_silu(q_ref, qp_ref, cw_ref[:, :MLSTM_WIDTH])
    k_all = conv_silu(k_ref, kp_ref, cw_ref[:, MLSTM_WIDTH:])

    row = lax.broadcasted_iota(jnp.int32, (L, L), 0)
    col = lax.broadcasted_iota(jnp.int32, (L, L), 1)
    incl = col <= row
    eye = (col == row).astype(F32)
    gates = gates_ref[...] + bif_ref[...]
    b_all = _dot(incl.astype(F32), jax.nn.log_sigmoid(gates), HIGHEST)
    to_row = lambda c: jnp.sum(c * eye, axis=0, keepdims=True)

    heads = range(MLSTM_HEADS)
    sls = [slice(h * D, (h + 1) * D) for h in heads]
    q = [q_all[:, sl] * (D ** -0.5) for sl in sls]
    k = [k_all[:, sl] for sl in sls]
    v = [v_ref[:, sl] for sl in sls]
    i_col = [gates[:, h:h + 1] for h in heads]
    b_col = [b_all[:, MLSTM_HEADS + h:MLSTM_HEADS + h + 1] for h in heads]
    m_prev = [m_ref[h] for h in heads]
    c_b = [_bf(c_ref[h]) for h in heads]
    d_mat = [jnp.where(incl, b_col[h] - to_row(b_col[h]) + to_row(i_col[h]), -jnp.inf) for h in heads]
    inter = [b_col[h] + m_prev[h] for h in heads]
    m_row = [jnp.maximum(inter[h], jnp.max(d_mat[h], axis=-1, keepdims=True)) for h in heads]
    w_inter = [jnp.exp(inter[h] - m_row[h]) for h in heads]
    q_b = [_bf(x) for x in q]
    k_b = [_bf(x) for x in k]
    s = [_dot_nt(q_b[h], k_b[h]) * jnp.exp(d_mat[h] - m_row[h]) for h in heads]
    qc = [_dot_nt(q_b[h], c_b[h]) for h in heads]
    sv = [_dot(_bf(s[h]), _bf(v[h])) for h in heads]
    b_last = [b_col[h][L - 1:L, :] for h in heads]
    d_state = [b_last[h] - b_col[h] + i_col[h] for h in heads]
    m_new = [jnp.maximum(b_last[h] + m_prev[h], jnp.max(d_state[h], axis=0, keepdims=True)) for h in heads]
    w_state = [jnp.exp(d_state[h] - m_new[h]) for h in heads]
    w_carry = [jnp.exp(b_last[h] + m_prev[h] - m_new[h]) for h in heads]
    kv = [_dot_tn(_bf(w_state[h] * v[h]), k_b[h]) for h in heads]
    for h in heads:
        n_vec = n_ref[h]
        num = sv[h] + w_inter[h] * qc[h]
        den = jnp.sum(s[h], axis=-1, keepdims=True) + w_inter[h] * jnp.sum(q[h] * n_vec, axis=-1, keepdims=True)
        hh = num / jnp.maximum(jnp.abs(den), jnp.exp(-m_row[h]))
        c_ref[h] = w_carry[h] * c_ref[h] + kv[h]
        n_ref[h] = w_carry[h] * n_vec + jnp.sum(w_state[h] * k[h], axis=0, keepdims=True)
        m_ref[h] = m_new[h]
        hn = hh * lax.rsqrt(jnp.mean(hh * hh, axis=-1, keepdims=True) + NORM_EPS) * ng_ref[:, sls[h]]
        out_ref[:, sls[h]] = (hn * jax.nn.sigmoid(o_ref[:, sls[h]])).astype(out_ref.dtype)


def _mlstm(proj, q_blk, v_blk, o_blk, gates_blk, conv_w, b_if, norm_g, out_dtype=F32):
    t = proj.shape[0]
    L, W = CHUNK, MLSTM_WIDTH
    bif = jnp.pad(b_if, (0, LORA_PAD - b_if.shape[0])).reshape(1, LORA_PAD)
    prev = lambda blk: pl.BlockSpec((PREV_ROWS, W), lambda i: (jnp.maximum(i * (L // PREV_ROWS) - 1, 0), blk))
    return pl.pallas_call(
        _mlstm_body,
        grid=(t // L,),
        in_specs=[
            pl.BlockSpec((L, W), lambda i: (i, q_blk)),
            pl.BlockSpec((L, W), lambda i: (i, q_blk + 1)),
            prev(q_blk), prev(q_blk + 1),
            pl.BlockSpec((MLSTM_CONV, 2 * W), lambda i: (0, 0)),
            pl.BlockSpec((L, W), lambda i: (i, v_blk)),
            pl.BlockSpec((L, W), lambda i: (i, o_blk)),
            pl.BlockSpec((L, LORA_PAD), lambda i: (i, gates_blk)),
            pl.BlockSpec((1, LORA_PAD), lambda i: (0, 0)),
            pl.BlockSpec((1, W), lambda i: (0, 0)),
        ],
        out_specs=pl.BlockSpec((L, W), lambda i: (i, 0)),
        out_shape=jax.ShapeDtypeStruct((t, W), out_dtype),
        scratch_shapes=[pltpu.VMEM((MLSTM_HEADS, MLSTM_HEAD_DIM, MLSTM_HEAD_DIM), F32),
                        pltpu.VMEM((MLSTM_HEADS, 1, MLSTM_HEAD_DIM), F32),
                        pltpu.VMEM((MLSTM_HEADS, 1, 1), F32)],
        compiler_params=_params("arbitrary"),
    )(proj, proj, proj, proj, conv_w, proj, proj, proj, bif, norm_g.reshape(1, W))


def _fox_qkv_prep_body(x_ref, g_ref, o_ref):
    D = FOX_HEAD_DIM

    @pl.when(pl.program_id(1) < 2)
    def _():
        for h in range(FOX_HEADS):
            sl = slice(h * D, (h + 1) * D)
            x = x_ref[:, sl]
            y = x * lax.rsqrt(jnp.mean(x * x, axis=-1, keepdims=True) + NORM_EPS) * g_ref[...]
            o_ref[:, sl] = y.astype(o_ref.dtype)

    @pl.when(pl.program_id(1) == 2)
    def _():
        o_ref[...] = x_ref[...].astype(o_ref.dtype)


def _fox_qkv_prep(proj, qn_g, kn_g, tm=512):
    t = proj.shape[0]
    W = FOX_HEADS * FOX_HEAD_DIM
    gains = jnp.stack([qn_g * (FOX_HEAD_DIM ** -0.5 * LOG2E), kn_g, jnp.ones_like(kn_g)]).reshape(3, 1, FOX_HEAD_DIM)
    return pl.pallas_call(
        _fox_qkv_prep_body,
        grid=(t // tm, 3),
        in_specs=[pl.BlockSpec((tm, W), lambda i, j: (i, j)),
                  pl.BlockSpec((None, 1, FOX_HEAD_DIM), lambda i, j: (j, 0, 0))],
        out_specs=pl.BlockSpec((tm, W), lambda i, j: (i, j)),
        out_shape=jax.ShapeDtypeStruct((t, 3 * W), BF16),
        compiler_params=_params("arbitrary", "arbitrary"),
    )(proj, gains)


def _log_forget_cumsum_body(f_ref, b_ref, o_ref, carry_ref):
    tm = f_ref.shape[0]

    @pl.when(pl.program_id(0) == 0)
    def _():
        carry_ref[...] = jnp.zeros_like(carry_ref)

    row = lax.broadcasted_iota(jnp.int32, (tm, tm), 0)
    col = lax.broadcasted_iota(jnp.int32, (tm, tm), 1)
    log_f = jax.nn.log_sigmoid(f_ref[...] + b_ref[...])
    cum = _dot((col <= row).astype(F32), log_f, HIGHEST) + carry_ref[...]
    o_ref[...] = cum * LOG2E
    carry_ref[...] = cum[tm - 1:tm, :]


def _log_forget_cumsum(f_pre, b_f, tm=512):
    t, nh = f_pre.shape
    return pl.pallas_call(
        _log_forget_cumsum_body,
        grid=(t // tm,),
        in_specs=[pl.BlockSpec((tm, nh), lambda i: (i, 0)), pl.BlockSpec((1, nh), lambda i: (0, 0))],
        out_specs=pl.BlockSpec((tm, nh), lambda i: (i, 0)),
        out_shape=jax.ShapeDtypeStruct((t, nh), F32),
        scratch_shapes=[pltpu.VMEM((1, nh), F32)],
        compiler_params=_params("arbitrary"),
    )(f_pre, b_f.reshape(1, nh))


def _fox_body(q_ref, k_ref, v_ref, g_ref, cq_ref, ck_ref, o_ref, sa_ref, sb_ref, va_ref, acc_ref, *, tq, tk, ts):
    qi = pl.program_id(1)
    n_sub = tq // ts
    subs = range(n_sub)
    rows = [pl.ds(i * ts, ts) for i in subs]

    def fill(dst_ref, kb):
        start = pl.multiple_of(kb * tk, tk)
        for i in subs:
            dst_ref[rows[i], :] = _dot_nt(q_ref[rows[i], :], k_ref[pl.ds(start, tk), :])

    def absorb(src_ref, kb, carry, diagonal):
        start = pl.multiple_of(kb * tk, tk)
        out = []
        for i in subs:
            m_prev = carry[i]
            width = (i + 1) * ts if diagonal else tk
            s = src_ref[rows[i], :width] + cq_ref[rows[i], :] - ck_ref[kb][:, :width]
            if diagonal:
                row = i * ts + lax.broadcasted_iota(jnp.int32, (ts, width), 0)
                col = lax.broadcasted_iota(jnp.int32, (ts, width), 1)
                s = jnp.where(col <= row, s, -jnp.inf)
            m_new = jnp.maximum(m_prev, jnp.max(s, axis=-1, keepdims=True))
            p = jnp.exp2(s - m_new)
            acc_ref[rows[i], :] = (jnp.exp2(m_prev - m_new) * acc_ref[rows[i], :]
                                   + _dot(_bf(p), va_ref[pl.ds(start, width), :]))
            out.append(m_new)
        return tuple(out)

    def pair(pi, carry):
        fill(sb_ref, 2 * pi + 1)
        carry = absorb(sa_ref, 2 * pi, carry, False)
        fill(sa_ref, 2 * pi + 2)
        return absorb(sb_ref, 2 * pi + 1, carry, False)

    D = FOX_HEAD_DIM

    @pl.when(qi == 0)
    def _():
        va_ref[:, :D] = v_ref[...]
        va_ref[:, D:] = jnp.ones((va_ref.shape[0], D), va_ref.dtype)

    acc_ref[...] = jnp.zeros_like(acc_ref)
    init = tuple(jnp.full((ts, 1), -jnp.inf, F32) for _ in range(n_sub))
    fill(sa_ref, 0)
    carry = lax.fori_loop(0, qi // 2, pair, init)

    def odd_tail(carry):
        fill(sb_ref, qi)
        carry = absorb(sa_ref, qi - 1, carry, False)
        return absorb(sb_ref, qi, carry, True)

    lax.cond(qi % 2 == 1, odd_tail, lambda c: absorb(sa_ref, qi, c, True), carry)
    for i in range(n_sub):
        acc = acc_ref[rows[i], :]
        o_ref[rows[i], :] = (acc[:, :D] / acc[:, D:] * jax.nn.sigmoid(g_ref[rows[i], :])).astype(o_ref.dtype)


def _fox_attention(qkv, proj, g_blk0, cum, tq=512):
    t = qkv.shape[0]
    D, H = FOX_HEAD_DIM, FOX_HEADS
    tk = tq
    cum_t = cum.T
    cq = cum_t.reshape(H, t, 1)
    ck = cum_t.reshape(H, t // tk, 1, tk)
    return pl.pallas_call(
        functools.partial(_fox_body, tq=tq, tk=tk, ts=FOX_SUB_TILE),
        grid=(H, t // tq),
        in_specs=[
            pl.BlockSpec((tq, D), lambda h, qi: (qi, h)),
            pl.BlockSpec((t, D), lambda h, qi: (0, H + h)),
            pl.BlockSpec((t, D), lambda h, qi: (0, 2 * H + h)),
            pl.BlockSpec((tq, D), lambda h, qi: (qi, g_blk0 + h)),
            pl.BlockSpec((None, tq, 1), lambda h, qi: (h, qi, 0)),
            pl.BlockSpec((None, t // tk, 1, tk), lambda h, qi: (h, 0, 0, 0)),
        ],
        out_specs=pl.BlockSpec((tq, D), lambda h, qi: (qi, h)),
        out_shape=jax.ShapeDtypeStruct((t, H * D), BF16),
        scratch_shapes=[pltpu.VMEM((tq, tk), F32), pltpu.VMEM((tq, tk), F32), pltpu.VMEM((t, 2 * D), BF16),
                        pltpu.VMEM((tq, 2 * D), F32)],
        compiler_params=_params("arbitrary", "arbitrary"),
    )(qkv, qkv, qkv, proj, cq, ck)


def _xattn_body(q_ref, k_ref, v_ref, o_ref):
    s = _dot_nt(q_ref[...], k_ref[...]) * (XATTN_HEAD_DIM ** -0.5)
    p = jnp.exp(s - jnp.max(s, axis=-1, keepdims=True))
    o = _dot(p.astype(BF16), v_ref[...]) / jnp.sum(p, axis=-1, keepdims=True)
    o_ref[...] = o.astype(o_ref.dtype)


def _cross_attention(q, kv, tq=512):
    t = q.shape[0]
    n_mem = kv.shape[0]
    D, H = XATTN_HEAD_DIM, XATTN_HEADS
    return pl.pallas_call(
        _xattn_body,
        grid=(t // tq, H),
        in_specs=[pl.BlockSpec((tq, D), lambda i, h: (i, h)),
                  pl.BlockSpec((n_mem, D), lambda i, h: (0, h)),
                  pl.BlockSpec((n_mem, D), lambda i, h: (0, H + h))],
        out_specs=pl.BlockSpec((tq, D), lambda i, h: (i, h)),
        out_shape=jax.ShapeDtypeStruct((t, H * D), BF16),
        compiler_params=_params("arbitrary", "arbitrary"),
    )(q, kv, kv)


def _swiglu_up_body(x_ref, g_ref, wg_ref, wu_ref, o_ref, xn_ref):
    @pl.when(pl.program_id(1) == 0)
    def _():
        x = x_ref[...]
        xn_ref[...] = _bf(x * lax.rsqrt(jnp.mean(x * x, axis=-1, keepdims=True) + NORM_EPS) * g_ref[...])

    xn = xn_ref[...]
    a = _dot(xn, _bf(wg_ref[...]))
    b = _dot(xn, _bf(wu_ref[...]))
    o_ref[...] = (a * jax.nn.sigmoid(a) * b).astype(o_ref.dtype)


def _swiglu_up(x, norm_g, w_gate, w_up, tm=MM_ROW_TILE, tf=MM_COL_TILE):
    t, d = x.shape
    f = w_gate.shape[1]
    tm = min(tm, t)
    return pl.pallas_call(
        _swiglu_up_body,
        grid=(t // tm, f // tf),
        in_specs=[pl.BlockSpec((tm, d), lambda i, j: (i, 0)),
                  pl.BlockSpec((1, d), lambda i, j: (0, 0)),
                  pl.BlockSpec((d, tf), lambda i, j: (0, j)),
                  pl.BlockSpec((d, tf), lambda i, j: (0, j))],
        out_specs=pl.BlockSpec((tm, tf), lambda i, j: (i, j)),
        out_shape=jax.ShapeDtypeStruct((t, f), BF16),
        scratch_shapes=[pltpu.VMEM((tm, d), BF16)],
        compiler_params=_params("arbitrary", "arbitrary"),
    )(x, norm_g.reshape(1, d), w_gate, w_up)


def _mm_acc_body(x_ref, w_ref, res_ref, o_ref):
    @pl.when(pl.program_id(2) == 0)
    def _():
        o_ref[...] = res_ref[...]

    o_ref[...] += _dot(x_ref[...], _bf(w_ref[...]))


def _mm_acc(x, w, res, tm=MM_ROW_TILE, tn=MM_ROW_TILE, tk=MM_COL_TILE):
    t, k = x.shape
    n = w.shape[1]
    tm = min(tm, t)
    return pl.pallas_call(
        _mm_acc_body,
        grid=(t // tm, n // tn, k // tk),
        in_specs=[pl.BlockSpec((tm, tk), lambda i, j, kk: (i, kk)),
                  pl.BlockSpec((tk, tn), lambda i, j, kk: (kk, j)),
                  pl.BlockSpec((tm, tn), lambda i, j, kk: (i, j))],
        out_specs=pl.BlockSpec((tm, tn), lambda i, j, kk: (i, j)),
        out_shape=jax.ShapeDtypeStruct((t, n), F32),
        compiler_params=_params("arbitrary", "arbitrary", "arbitrary"),
    )(x, w, res)


def _rmsnorm_body(x_ref, g_ref, o_ref):
    x = x_ref[...]
    o_ref[...] = x * lax.rsqrt(jnp.mean(x * x, axis=-1, keepdims=True) + NORM_EPS) * g_ref[...]


def _rmsnorm(x, g, tm=512):
    t, d = x.shape
    return pl.pallas_call(
        _rmsnorm_body,
        grid=(t // tm,),
        in_specs=[pl.BlockSpec((tm, d), lambda i: (i, 0)), pl.BlockSpec((1, d), lambda i: (0, 0))],
        out_specs=pl.BlockSpec((tm, d), lambda i: (i, 0)),
        out_shape=jax.ShapeDtypeStruct((t, d), F32),
        compiler_params=_params("arbitrary"),
    )(x, g.reshape(1, d))


def _pad_cols(w, n=LORA_PAD):
    return jnp.pad(w, [(0, 0)] * (w.ndim - 1) + [(0, n - w.shape[-1])])


def _pack_ab_columns(w):
    a_cols = 3 * RWKV_WIDTH + DECAY_LORA + AAA_LORA + GATE_LORA
    a, b = w[..., :a_cols], w[..., a_cols:]
    o1 = 3 * RWKV_WIDTH
    o2 = o1 + DECAY_LORA
    o3 = o2 + AAA_LORA
    return jnp.concatenate([
        a[..., :o1], b[..., :4 * MLSTM_WIDTH],
        _pad_cols(a[..., o1:o2]), _pad_cols(a[..., o2:o3]), a[..., o3:],
        _pad_cols(b[..., 4 * MLSTM_WIDTH:])], axis=-1)


def _rwkv_mlstm_mixer(x, norm_g, w_in, mu, w0, w_up, a0, a_up, g_up, k_k, k_a, r_k, ln_g, ln_b,
                      conv_qk, b_if, mh_g, w_out):
    proj = _norm_mm(x, _pack_ab_columns(w_in), norm_g=norm_g)
    mu_p = _pack_ab_columns(jnp.concatenate([mu, jnp.zeros((4 * MLSTM_WIDTH + 2 * MLSTM_HEADS,), F32)]))
    o1 = 3 * RWKV_WIDTH
    o_lora = o1 + 4 * MLSTM_WIDTH
    lora_w = 2 * LORA_PAD + GATE_LORA
    y_a = _rwkv7(proj, o_lora, mu_p[:o1], mu_p[o_lora:o_lora + lora_w], w0, w_up, a0, a_up, g_up, k_k, k_a, r_k,
                 ln_g, ln_b, out_dtype=BF16)
    y_b = _mlstm(proj, o1 // MLSTM_WIDTH, o1 // MLSTM_WIDTH + 2, o1 // MLSTM_WIDTH + 3,
                 (o_lora + lora_w) // LORA_PAD, conv_qk, b_if, mh_g, out_dtype=BF16)

    x = _norm_mm(y_a, w_out[:RWKV_WIDTH], res=x)
    return _norm_mm(y_b, w_out[RWKV_WIDTH:], res=x)


def _fox_layer(x, norm_g, w_in, b_f, qn_g, kn_g, w_out):
    W = FOX_HEADS * FOX_HEAD_DIM
    proj = _norm_mm(x, w_in, norm_g=norm_g)
    qkv = _fox_qkv_prep(proj, qn_g, kn_g)
    cum = _log_forget_cumsum(proj[:, 4 * W:], b_f)
    o = _fox_attention(qkv, proj, 3 * FOX_HEADS, cum)
    return _norm_mm(o, w_out, res=x)


def _xattn_layer(x, norm_g, kv, wq, wo):
    q = _norm_mm(x, wq, norm_g=norm_g, out_dtype=BF16)
    return _norm_mm(_cross_attention(q, kv), wo, res=x)


def _moe_experts_body(tile_e_ref, n_valid_ref, tile_rows_ref, x_ref, g_ref, wg_ref, wu_ref, wd_ref, gate_ref,
                      o_ref, xn_ref):
    i = pl.program_id(0)
    j = pl.program_id(1)

    @pl.when(j == 0)
    def _():
        o_ref[...] = jnp.zeros_like(o_ref)
        x = x_ref[...]
        xn_ref[...] = _bf(x * lax.rsqrt(jnp.mean(x * x, axis=-1, keepdims=True) + NORM_EPS) * g_ref[...])

    wg, wu, wd = _bf(wg_ref[...]), _bf(wu_ref[...]), _bf(wd_ref[...])
    n_sub = (tile_rows_ref[i] + MOE_SUB_ROWS - 1) // MOE_SUB_ROWS
    for n in range(1, x_ref.shape[0] // MOE_SUB_ROWS + 1):
        rows = pl.ds(0, n * MOE_SUB_ROWS)

        @pl.when(n_sub == n)
        def _():
            x = xn_ref[rows, :]
            a = _dot(x, wg)
            b = _dot(x, wu)
            h = a * jax.nn.sigmoid(a) * b
            o_ref[rows, :] += _dot(_bf(h), wd)

    @pl.when(j == pl.num_programs(1) - 1)
    def _():
        o_ref[...] = o_ref[...] * gate_ref[...]


def _moe_experts(xg, norm_g, gate_at, tile_e, n_valid, tile_rows, w_gate, w_up, w_down, tm, tf=256):
    p, d = xg.shape
    f = w_gate.shape[2]
    nf = f // tf

    def expert(i, te, nv):
        return te[jnp.minimum(i, nv[0] - 1)]

    def fblock(i, j, nv):
        return jnp.where(i < nv[0], j, nf - 1)

    grid_spec = pltpu.PrefetchScalarGridSpec(
        num_scalar_prefetch=3,
        grid=(p // tm, nf),
        in_specs=[
            pl.BlockSpec((tm, d), lambda i, j, te, nv, tr: (i, 0), pipeline_mode=pl.Buffered(1)),
            pl.BlockSpec((1, d), lambda i, j, te, nv, tr: (0, 0)),
            pl.BlockSpec((None, d, tf), lambda i, j, te, nv, tr: (expert(i, te, nv), 0, fblock(i, j, nv))),
            pl.BlockSpec((None, d, tf), lambda i, j, te, nv, tr: (expert(i, te, nv), 0, fblock(i, j, nv))),
            pl.BlockSpec((None, tf, d), lambda i, j, te, nv, tr: (expert(i, te, nv), fblock(i, j, nv), 0)),
            pl.BlockSpec((tm, 1), lambda i, j, te, nv, tr: (i, 0)),
        ],
        out_specs=pl.BlockSpec((tm, d), lambda i, j, te, nv, tr: (i, 0)),
        scratch_shapes=[pltpu.VMEM((tm, d), BF16)],
    )
    return pl.pallas_call(
        _moe_experts_body,
        grid_spec=grid_spec,
        out_shape=jax.ShapeDtypeStruct((p, d), F32),
        compiler_params=_params("arbitrary", "arbitrary"),
    )(tile_e, n_valid, tile_rows, xg, norm_g.reshape(1, d), w_gate, w_up, w_down, gate_at)


MOE_TILE = 1024
MOE_SUB_ROWS = 256


def _moe_layer(x, norm_g, w_router, w_gate, w_up, w_down):
    t = x.shape[0]
    tm = MOE_TILE
    logits = _norm_mm(x, _pad_cols(w_router), norm_g=norm_g, tm=512, mxu_dtype=F32, precision=HIGHEST)
    top_val, top_idx = lax.top_k(logits[:, :N_EXPERTS], TOP_K)
    gates = jax.nn.softmax(top_val, axis=-1)
    flat_e = top_idx.reshape(-1)
    onehot = (flat_e[:, None] == jnp.arange(N_EXPERTS)[None, :]).astype(jnp.int32)
    ranks = jnp.cumsum(onehot, axis=0) - onehot
    counts = jnp.sum(onehot, axis=0)
    padded = (counts + tm - 1) // tm * tm
    pad_end = jnp.cumsum(padded)
    dest = (pad_end - padded)[flat_e] + jnp.sum(ranks * onehot, axis=1)
    n_rows = TOP_K * t + N_EXPERTS * tm
    tok_at = (jnp.arange(n_rows, dtype=jnp.int32) % t).at[dest].set(jnp.arange(TOP_K * t, dtype=jnp.int32) // TOP_K)
    gate_at = jnp.zeros((n_rows,), F32).at[dest].set(gates.reshape(-1))
    tile_start = jnp.arange(n_rows // tm, dtype=jnp.int32) * tm
    tile_e = jnp.minimum(jnp.sum((tile_start[:, None] >= pad_end[None, :]).astype(jnp.int32), axis=1), N_EXPERTS - 1)
    n_valid = (pad_end[-1:] // tm).astype(jnp.int32)
    routed_end = pad_end - padded + counts
    tile_rows = jnp.clip(routed_end[tile_e] - tile_start, 0, tm).astype(jnp.int32)
    xg = jnp.take(x, tok_at, axis=0, mode="clip")
    y = _moe_experts(xg, norm_g, gate_at.reshape(n_rows, 1), tile_e, n_valid, tile_rows, w_gate, w_up, w_down, tm)
    dest2 = dest.reshape(t, TOP_K)
    return x + jnp.take(y, dest2[:, 0], axis=0, mode="clip") + jnp.take(y, dest2[:, 1], axis=0, mode="clip")


def kernel(x, mem, mem_norm_g, final_norm_g, norm_mix_g, norm_xattn_g, norm_ffn_g,
           xa_wq, xa_wkv, xa_wo,
           ab_w_in, ab_mu, rw_w0, rw_w_up, rw_a0, rw_a_up, rw_g_up, rw_k_k, rw_k_a, rw_r_k,
           rw_ln_g, rw_ln_b, ml_conv, ml_b_if, ml_norm_g, ab_w_out,
           ffn_w_gate, ffn_w_up, ffn_w_down,
           fox_w_in, fox_b_f, fox_qn_g, fox_kn_g, fox_w_out,
           moe_router, moe_w_gate, moe_w_up, moe_w_down):
    bsz, seq, d = x.shape
    assert bsz == 1 and d == D_MODEL and seq % 512 == 0
    xs = x[0]
    mem_s = mem[0]
    depth = norm_mix_g.shape[0]
    for layer in range(depth):
        j = layer // 2
        if layer % 2 == 0:
            xs = _rwkv_mlstm_mixer(xs, norm_mix_g[layer], ab_w_in[j], ab_mu[j], rw_w0[j], rw_w_up[j], rw_a0[j],
                                   rw_a_up[j], rw_g_up[j], rw_k_k[j], rw_k_a[j], rw_r_k[j], rw_ln_g[j],
                                   rw_ln_b[j], ml_conv[j], ml_b_if[j], ml_norm_g[j], ab_w_out[j])
        else:
            xs = _fox_layer(xs, norm_mix_g[layer], fox_w_in[j], fox_b_f[j], fox_qn_g[j], fox_kn_g[j],
                            fox_w_out[j])
        kv = _norm_mm(mem_s, xa_wkv[layer], norm_g=mem_norm_g, tm=mem_s.shape[0], out_dtype=BF16)
        xs = _xattn_layer(xs, norm_xattn_g[layer], kv, xa_wq[layer], xa_wo[layer])
        if layer % 2 == 0:
            hidden = _swiglu_up(xs, norm_ffn_g[layer], ffn_w_gate[j], ffn_w_up[j])
            xs = _mm_acc(hidden, ffn_w_down[j], xs)
        else:
            xs = _moe_layer(xs, norm_ffn_g[layer], moe_router[j], moe_w_gate[j], moe_w_up[j], moe_w_down[j])
    return _rmsnorm(xs, final_norm_g)[None]
```
